```python
import jax, jax.numpy as jnp
from jax import lax
import numpy as np

D_MODEL = 4096
BATCH = 4
SEQ = 2048
DEPTH = 2

N_META = 16
CHUNK = 16
EXPAND = 2
D_INNER = EXPAND * D_MODEL
N_MIXERS = 2
HG_DK = 128
HG_HEADS = D_INNER // HG_DK
HG_DV = D_INNER // HG_HEADS
GLA_HEADS = 8
GLA_KW = D_INNER // 2
GLA_DK = GLA_KW // GLA_HEADS
GLA_DV = D_INNER // GLA_HEADS
GLA_RANK = 16
GLA_GATE_NORM = 16.0
ALPHA = (2.0 * DEPTH) ** 0.25
BETA = (8.0 * DEPTH) ** -0.25
LN_EPS = 1e-5
RMS_EPS = 1e-6

kernel_name = 'hgrn2_gla_interleaved_deepnorm_meta'


def layer_norm(x, g, b):
    xf = x.astype(jnp.float32)
    mu = jnp.mean(xf, axis=-1, keepdims=True)
    var = jnp.mean(jnp.square(xf - mu), axis=-1, keepdims=True)
    return ((xf - mu) * lax.rsqrt(var + LN_EPS) * g.astype(jnp.float32) + b.astype(jnp.float32)).astype(x.dtype)


def head_rms_norm(o, g):
    B, T, H, dv = o.shape
    ms = jnp.mean(jnp.square(o), axis=-1, keepdims=True)
    return (o * lax.rsqrt(ms + RMS_EPS)).reshape(B, T, H * dv) * g.astype(jnp.float32)


def chunked_gated_linear_attention(q, k, v, log_g):
    B, T, H, dk = q.shape
    dv = v.shape[-1]
    nc = T // CHUNK

    def to_chunks(a):
        return a.reshape(B, nc, CHUNK, H, a.shape[-1]).transpose(1, 0, 3, 2, 4)

    q, k, v, log_g = (to_chunks(a) for a in (q, k, v, log_g))
    b = jnp.cumsum(log_g, axis=-2)
    b_last = b[..., -1:, :]
    q_dec = q * jnp.exp(b)
    k_inv = k * jnp.exp(-b)
    k_dec = k * jnp.exp(b_last - b)
    causal = jnp.tril(jnp.ones((CHUNK, CHUNK), dtype=bool))
    scores = jnp.where(causal, jnp.einsum('nbhid,nbhjd->nbhij', q_dec, k_inv), 0.0)
    o_intra = jnp.einsum('nbhij,nbhjv->nbhiv', scores, v)
    g_last = jnp.exp(b_last[..., 0, :])

    def step(S, xs):
        qd, kd, vc, gl = xs
        o = jnp.einsum('bhid,bhdv->bhiv', qd, S)
        S = gl[..., None] * S + jnp.einsum('bhjd,bhjv->bhdv', kd, vc)
        return S, o

    S0 = jnp.zeros((B, H, dk, dv), dtype=q.dtype)
    _, o_inter = lax.scan(step, S0, (q_dec, k_dec, v, g_last))
    o = o_intra + o_inter
    return o.transpose(1, 0, 3, 2, 4).reshape(B, T, H, dv)


def hgrn2_mixer(x, w_in, b_f, lb, norm_g, w_out):
    B, T, _ = x.shape
    q, f, i, z = jnp.split(x @ w_in, 4, axis=-1)
    q = jax.nn.silu(q.astype(jnp.float32))
    fg = lb + (1.0 - lb) * jax.nn.sigmoid((f + b_f).astype(jnp.float32))
    k = 1.0 - fg
    log_g = jnp.log(fg)
    o = chunked_gated_linear_attention(
        q.reshape(B, T, HG_HEADS, HG_DK), k.reshape(B, T, HG_HEADS, HG_DK),
        i.astype(jnp.float32).reshape(B, T, HG_HEADS, HG_DV), log_g.reshape(B, T, HG_HEADS, HG_DK))
    y = head_rms_norm(o, norm_g) * jax.nn.silu(z.astype(jnp.float32))
    return y.astype(x.dtype) @ w_out


def gla_mixer(x, w_in, w_g1, w_g2, b_g, norm_g, w_out):
    B, T, _ = x.shape
    q, k, v, z = jnp.split(x @ w_in, [GLA_KW, 2 * GLA_KW, 2 * GLA_KW + D_INNER], axis=-1)
    log_g = jax.nn.log_sigmoid(((x @ w_g1) @ w_g2 + b_g).astype(jnp.float32)) / GLA_GATE_NORM
    q = q.astype(jnp.float32) * (GLA_DK ** -0.5)
    o = chunked_gated_linear_attention(
        q.reshape(B, T, GLA_HEADS, GLA_DK), k.astype(jnp.float32).reshape(B, T, GLA_HEADS, GLA_DK),
        v.astype(jnp.float32).reshape(B, T, GLA_HEADS, GLA_DV), log_g.reshape(B, T, GLA_HEADS, GLA_DK))
    y = head_rms_norm(o, norm_g) * jax.nn.silu(z.astype(jnp.float32))
    return y.astype(x.dtype) @ w_out


def setup_inputs(seed: int = 0) -> dict:
    key = jax.random.key(seed)
    ks = jax.random.split(key, 24)
    n = lambda k, shape, s: jax.random.normal(k, shape, jnp.float32) * s
    sd = D_MODEL ** -0.5
    x = n(ks[0], (BATCH, SEQ, D_MODEL), 1.0)
    meta = n(ks[1], (N_META, D_MODEL), 1.0)
    lb_logits = n(ks[2], (DEPTH + 1, HG_HEADS * HG_DK), 0.1)
    l0_w_in = jnp.concatenate([
        n(ks[3], (D_MODEL, D_INNER), sd),
        n(ks[4], (D_MODEL, D_INNER), sd),
        n(ks[5], (D_MODEL, D_INNER), sd * BETA),
        n(ks[6], (D_MODEL, D_INNER), sd),
    ], axis=1)
    l0_b_f = n(ks[7], (D_INNER,), 0.01)
    l0_norm_g = 1.0 + n(ks[8], (D_INNER,), 0.01)
    l0_w_out = n(ks[9], (D_INNER, D_MODEL), D_INNER ** -0.5 * BETA)
    l0_ln_g = 1.0 + n(ks[10], (D_MODEL,), 0.01)
    l0_ln_b = n(ks[11], (D_MODEL,), 0.01)
    l1_w_in = jnp.concatenate([
        n(ks[12], (D_MODEL, GLA_KW), sd),
        n(ks[13], (D_MODEL, GLA_KW), sd),
        n(ks[14], (D_MODEL, D_INNER), sd * BETA),
        n(ks[15], (D_MODEL, D_INNER), sd),
    ], axis=1)
    l1_w_g1 = n(ks[16], (D_MODEL, GLA_RANK), sd)
    l1_w_g2 = n(ks[17], (GLA_RANK, GLA_KW), GLA_RANK ** -0.5)
    l1_b_g = n(ks[18], (GLA_KW,), 0.01)
    l1_norm_g = 1.0 + n(ks[19], (D_INNER,), 0.01)
    l1_w_out = n(ks[20], (D_INNER, D_MODEL), D_INNER ** -0.5 * BETA)
    l1_ln_g = 1.0 + n(ks[21], (D_MODEL,), 0.01)
    l1_ln_b = n(ks[22], (D_MODEL,), 0.01)
    return {'x': x, 'meta': meta, 'lb_logits': lb_logits,
            'l0_w_in': l0_w_in, 'l0_b_f': l0_b_f, 'l0_norm_g': l0_norm_g, 'l0_w_out': l0_w_out,
            'l0_ln_g': l0_ln_g, 'l0_ln_b': l0_ln_b,
            'l1_w_in': l1_w_in, 'l1_w_g1': l1_w_g1, 'l1_w_g2': l1_w_g2, 'l1_b_g': l1_b_g,
            'l1_norm_g': l1_norm_g, 'l1_w_out': l1_w_out, 'l1_ln_g': l1_ln_g, 'l1_ln_b': l1_ln_b}


def reference(x, meta, lb_logits,
              l0_w_in, l0_b_f, l0_norm_g, l0_w_out, l0_ln_g, l0_ln_b,
              l1_w_in, l1_w_g1, l1_w_g2, l1_b_g, l1_norm_g, l1_w_out, l1_ln_g, l1_ln_b):
    B = x.shape[0]
    lb_all = jnp.cumsum(jax.nn.softmax(lb_logits.astype(jnp.float32), axis=0), axis=0)
    mixers = (
        lambda h, li: hgrn2_mixer(h, l0_w_in, l0_b_f, lb_all[li], l0_norm_g, l0_w_out),
        lambda h, li: gla_mixer(h, l1_w_in, l1_w_g1, l1_w_g2, l1_b_g, l1_norm_g, l1_w_out),
    )
    post_norms = ((l0_ln_g, l0_ln_b), (l1_ln_g, l1_ln_b))
    meta_b = jnp.broadcast_to(meta[None].astype(x.dtype), (B, N_META, D_MODEL))
    h = jnp.concatenate([meta_b, x], axis=1)
    for li in range(DEPTH):
        g, b = post_norms[li]
        h = layer_norm(ALPHA * h + mixers[li % N_MIXERS](h, li), g, b)
    return h[:, N_META:]
```

```python
import functools

import numpy as np
import jax
import jax.numpy as jnp
from jax import lax
from jax.experimental import pallas as pl
from jax.experimental.pallas import tpu as pltpu

D_MODEL = 4096
SEQ = 2048
N_META = 16
T_TOT = SEQ + N_META
D_INNER = 2 * D_MODEL
HG_DK = 128
GLA_HEADS = 8
GLA_KW = D_INNER // 2
GLA_DK = GLA_KW // GLA_HEADS
GLA_DV = D_INNER // GLA_HEADS
GLA_RANK = 16
GLA_GATE_NORM = 16.0
DEPTH = 2
ALPHA = (2.0 * DEPTH) ** 0.25
LN_EPS = 1e-5
RMS_EPS = 1e-6

SUB = 16
BLK = 128
LEVELS = (32, 64, 128)
LANE = 128
VMEM_LIMIT_V7X = 60 * 1024 * 1024

F32 = jnp.float32
BF16 = jnp.bfloat16


def _dot(a, b):
    return jnp.dot(a, b, preferred_element_type=F32)


def _dot_nt(a, b):
    return lax.dot_general(a, b, (((1,), (1,)), ((), ())), preferred_element_type=F32)


def _dot_tn(a, b):
    return lax.dot_general(a, b, (((0,), (0,)), ((), ())), preferred_element_type=F32)


def _sigmoid(x):
    return 1.0 / (1.0 + jnp.exp(-x))


def _proj_kernel(a_ref, w_ref, o_ref, *, act, scale):
    acc = _dot(a_ref[...], w_ref[...].astype(BF16))
    if act == "silu":
        acc = acc * _sigmoid(acc)
    elif act == "scale":
        acc = acc * scale
    o_ref[...] = acc.astype(o_ref.dtype)


def _proj(a, w, col0, ncols, *, tm, tn, out_dtype, act="none", scale=1.0, name):
    m, k = a.shape
    assert m % tm == 0 and ncols % tn == 0 and col0 % tn == 0
    jb = col0 // tn
    return pl.pallas_call(
        functools.partial(_proj_kernel, act=act, scale=scale),
        out_shape=jax.ShapeDtypeStruct((m, ncols), out_dtype),
        grid=(m // tm, ncols // tn),
        in_specs=[
            pl.BlockSpec((tm, k), lambda i, j: (i, 0)),
            pl.BlockSpec((k, tn), lambda i, j: (0, j + jb)),
        ],
        out_specs=pl.BlockSpec((tm, tn), lambda i, j: (i, j)),
        compiler_params=pltpu.CompilerParams(
            dimension_semantics=("arbitrary", "arbitrary"),
            vmem_limit_bytes=VMEM_LIMIT_V7X),
        name=name,
    )(a, w)


def _oproj_kernel(a_ref, w_ref, o_ref, acc_ref):
    kk = pl.program_id(2)

    @pl.when(kk == 0)
    def _():
        acc_ref[...] = jnp.zeros_like(acc_ref)

    acc_ref[...] += _dot(a_ref[...], w_ref[...].astype(BF16))

    @pl.when(kk == pl.num_programs(2) - 1)
    def _():
        o_ref[...] = acc_ref[...].astype(o_ref.dtype)


def _oproj(a, w, *, tm, tn, tk, name):
    m, k = a.shape
    n = w.shape[1]
    assert m % tm == 0 and n % tn == 0 and k % tk == 0
    return pl.pallas_call(
        _oproj_kernel,
        out_shape=jax.ShapeDtypeStruct((m, n), BF16),
        grid=(m // tm, n // tn, k // tk),
        in_specs=[
            pl.BlockSpec((tm, tk), lambda i, j, kk: (i, kk)),
            pl.BlockSpec((tk, tn), lambda i, j, kk: (kk, j)),
        ],
        out_specs=pl.BlockSpec((tm, tn), lambda i, j, kk: (i, j)),
        scratch_shapes=[pltpu.VMEM((tm, tn), F32)],
        compiler_params=pltpu.CompilerParams(
            dimension_semantics=("arbitrary", "arbitrary", "arbitrary"),
            vmem_limit_bytes=VMEM_LIMIT_V7X),
        name=name,
    )(a, w)


def _ln_kernel(h_ref, br_ref, g_ref, b_ref, *o_refs):
    r = ALPHA * h_ref[...] + br_ref[...].astype(F32)
    mu = jnp.mean(r, axis=-1, keepdims=True)
    c = r - mu
    var = jnp.mean(c * c, axis=-1, keepdims=True)
    y = c * lax.rsqrt(var + LN_EPS) * g_ref[...] + b_ref[...]
    for o_ref in o_refs:
        o_ref[...] = y.astype(o_ref.dtype)


def _ln_all_rows(h, br, g, b, *, tr, name):
    m, d = h.shape
    assert m % tr == 0
    row = pl.BlockSpec((tr, d), lambda i: (i, 0))
    vec = pl.BlockSpec((1, d), lambda i: (0, 0))
    return pl.pallas_call(
        _ln_kernel,
        out_shape=(jax.ShapeDtypeStruct((m, d), F32), jax.ShapeDtypeStruct((m, d), BF16)),
        grid=(m // tr,),
        in_specs=[row, row, vec, vec],
        out_specs=(row, row),
        compiler_params=pltpu.CompilerParams(
            dimension_semantics=("arbitrary",), vmem_limit_bytes=VMEM_LIMIT_V7X),
        name=name,
    )(h, br, g.reshape(1, d), b.reshape(1, d))


def _ln_seq_rows(h, br, g, b, *, tr, name):
    bsz, _, d = h.shape
    assert SEQ % tr == 0
    row = pl.BlockSpec((None, tr, d), lambda bi, i: (bi, i, 0))
    vec = pl.BlockSpec((1, d), lambda bi, i: (0, 0))
    return pl.pallas_call(
        _ln_kernel,
        out_shape=jax.ShapeDtypeStruct((bsz, SEQ, d), F32),
        grid=(bsz, SEQ // tr),
        in_specs=[row, row, vec, vec],
        out_specs=row,
        compiler_params=pltpu.CompilerParams(
            dimension_semantics=("arbitrary", "arbitrary"), vmem_limit_bytes=VMEM_LIMIT_V7X),
        name=name,
    )(h, br, g.reshape(1, d), b.reshape(1, d))


def _decay_consts():
    c = BLK
    i = np.arange(c)[:, None]
    s = np.arange(c)[None, :]
    same_sub = (i // SUB) == (s // SUB)
    mats = [same_sub & (s <= i), s <= i, s > i]
    lvl = np.full((c, c), -1, np.int32)
    lvl[same_sub & (s <= i)] = 0
    for l, g in enumerate(LEVELS):
        half = g // 2
        same = (i // g) == (s // g)
        up = (i % g) >= half
        mid = (i // g) * g + half
        mats.append((same & up & (s >= mid) & (s <= i)) | (same & ~up & (s > i) & (s < mid)))
        lvl[same & up & ((s % g) < half)] = l + 1
    mstack = np.concatenate(mats, axis=0).astype(np.float32)
    return jnp.asarray(mstack, BF16), jnp.asarray(lvl)


def _block_scores_state(q, k, lg, v, mst_ref, lvl, st_ref, first):
    c = BLK
    lg_hi = lg.astype(BF16)
    lg_lo = (lg - lg_hi.astype(F32)).astype(BF16)

    def expo(l):
        m = mst_ref[l * c:(l + 1) * c, :]
        return _dot(m, lg_hi) + _dot(m, lg_lo)

    e0 = expo(0)
    sc = _dot_nt((q * jnp.exp(e0)).astype(BF16), (k * jnp.exp(-e0)).astype(BF16))
    sc = jnp.where(lvl == 0, sc, 0.0)
    for l in range(len(LEVELS)):
        x = jnp.exp(expo(3 + l))
        s_l = _dot_nt((q * x).astype(BF16), (k * x).astype(BF16))
        sc = jnp.where(lvl == l + 1, s_l, sc)
    o = _dot(sc.astype(BF16), v)

    e_b = expo(1)
    if not first:
        o = o + _dot_nt((q * jnp.exp(e_b)).astype(BF16), st_ref[...].astype(BF16))
    k_dec = (k * jnp.exp(expo(2))).astype(BF16)
    upd = _dot_tn(v, k_dec)
    if first:
        st_ref[...] = upd
    else:
        st_ref[...] = jnp.exp(e_b[c - 1:c, :]) * st_ref[...] + upd
    return o


def _norm_gate(o, ng, z):
    ms = jnp.mean(o * o, axis=-1, keepdims=True)
    return o * lax.rsqrt(ms + RMS_EPS) * ng * z


def _row_schedule(body_meta, body_main):
    body_meta()

    def step(i, carry):
        body_main(pl.multiple_of(i * BLK, BLK))
        return carry

    lax.fori_loop(0, SEQ // BLK, step, 0)


META_WIN0 = T_TOT - BLK
META_OFF = BLK - N_META


def _hgrn2_kernel(q_ref, f_ref, v_ref, z_ref, bf_ref, lbl_ref, ng_ref, mst_ref, lvl_ref,
                  y_ref, st_ref, *, heads):
    lbl = lbl_ref[...]
    e = jnp.exp(lbl - jnp.max(lbl, axis=0, keepdims=True))
    lb = e[0:1, :] / jnp.sum(e, axis=0, keepdims=True)
    lvl = lvl_ref[...]
    valid = lax.broadcasted_iota(jnp.int32, (BLK, 1), 0) >= META_OFF

    def block(r0, first):
        rows = pl.ds(r0, BLK)
        f = f_ref[rows, :] + bf_ref[...]
        fg = lb + (1.0 - lb) * _sigmoid(f)
        lg = jnp.log(fg)
        k = 1.0 - fg
        q = q_ref[rows, :].astype(F32)
        v = v_ref[rows, :]
        z = z_ref[rows, :].astype(F32)
        if first:
            lg = jnp.where(valid, lg, 0.0)
            k = jnp.where(valid, k, 0.0)
            v = jnp.where(valid, v, jnp.zeros_like(v))
        for h in range(heads):
            cs = slice(h * HG_DK, (h + 1) * HG_DK)
            o = _block_scores_state(q[:, cs], k[:, cs], lg[:, cs], v[:, cs],
                                    mst_ref, lvl, st_ref.at[h], first)
            y = _norm_gate(o, ng_ref[:, cs], z[:, cs]).astype(y_ref.dtype)
            if first:
                y_ref[pl.ds(SEQ, N_META), cs] = y[META_OFF:, :]
            else:
                y_ref[rows, cs] = y

    _row_schedule(lambda: block(META_WIN0, True), lambda r0: block(r0, False))


def _hgrn2_attention(q, f, v, z, b_f, lb_logits, norm_g, *, heads):
    bsz, t, n = q.shape
    hw = heads * HG_DK
    mst, lvl = _decay_consts()
    act = pl.BlockSpec((None, t, hw), lambda bi, g: (bi, 0, g))
    vec = pl.BlockSpec((1, hw), lambda bi, g: (0, g))
    return pl.pallas_call(
        functools.partial(_hgrn2_kernel, heads=heads),
        out_shape=jax.ShapeDtypeStruct((bsz, t, n), BF16),
        grid=(bsz, n // hw),
        in_specs=[act, act, act, act, vec,
                  pl.BlockSpec((lb_logits.shape[0], hw), lambda bi, g: (0, g)),
                  vec,
                  pl.BlockSpec(mst.shape, lambda bi, g: (0, 0)),
                  pl.BlockSpec(lvl.shape, lambda bi, g: (0, 0))],
        out_specs=act,
        scratch_shapes=[pltpu.VMEM((heads, HG_DK, HG_DK), F32)],
        compiler_params=pltpu.CompilerParams(
            dimension_semantics=("arbitrary", "arbitrary"), vmem_limit_bytes=VMEM_LIMIT_V7X),
        name="hgrn2_attention",
    )(q, f, v, z, b_f.reshape(1, n), lb_logits, norm_g.reshape(1, n), mst, lvl)


def _gla_kernel(q_ref, k_ref, v_ref, z_ref, g1_ref, wg2_ref, bg_ref, ng_ref, mst_ref, lvl_ref,
                y_ref, st_ref):
    lvl = lvl_ref[...]
    valid = lax.broadcasted_iota(jnp.int32, (BLK, 1), 0) >= META_OFF
    wg2 = wg2_ref[...].astype(BF16)

    def block(r0, first):
        rows = pl.ds(r0, BLK)
        u = _dot(g1_ref[rows, :].astype(BF16), wg2) + bg_ref[...]
        lg = (jnp.minimum(u, 0.0) - jnp.log(1.0 + jnp.exp(-jnp.abs(u)))) * (1.0 / GLA_GATE_NORM)
        q = q_ref[rows, :].astype(F32)
        k = k_ref[rows, :].astype(F32)
        v = v_ref[rows, :]
        z = z_ref[rows, :].astype(F32)
        if first:
            lg = jnp.where(valid, lg, 0.0)
            k = jnp.where(valid, k, 0.0)
            v = jnp.where(valid, v, jnp.zeros_like(v))
        o = _block_scores_state(q, k, lg, v, mst_ref, lvl, st_ref, first)
        y = _norm_gate(o, ng_ref[...], z).astype(y_ref.dtype)
        if first:
            y_ref[pl.ds(SEQ, N_META), :] = y[META_OFF:, :]
        else:
            y_ref[rows, :] = y

    _row_schedule(lambda: block(META_WIN0, True), lambda r0: block(r0, False))


def _gla_attention(q, k, v, z, g1, w_g2, b_g, norm_g):
    bsz, t, _ = q.shape
    mst, lvl = _decay_consts()
    qk = pl.BlockSpec((None, t, GLA_DK), lambda bi, h: (bi, 0, h))
    vz = pl.BlockSpec((None, t, GLA_DV), lambda bi, h: (bi, 0, h))
    return pl.pallas_call(
        _gla_kernel,
        out_shape=jax.ShapeDtypeStruct((bsz, t, D_INNER), BF16),
        grid=(bsz, GLA_HEADS),
        in_specs=[qk, qk, vz, vz,
                  pl.BlockSpec((None, t, LANE), lambda bi, h: (bi, 0, 0)),
                  pl.BlockSpec((LANE, GLA_DK), lambda bi, h: (0, h)),
                  pl.BlockSpec((1, GLA_DK), lambda bi, h: (0, h)),
                  pl.BlockSpec((1, GLA_DV), lambda bi, h: (0, h)),
                  pl.BlockSpec(mst.shape, lambda bi, h: (0, 0)),
                  pl.BlockSpec(lvl.shape, lambda bi, h: (0, 0))],
        out_specs=vz,
        scratch_shapes=[pltpu.VMEM((GLA_DV, GLA_DK), F32)],
        compiler_params=pltpu.CompilerParams(
            dimension_semantics=("arbitrary", "arbitrary"), vmem_limit_bytes=VMEM_LIMIT_V7X),
        name="gla_attention",
    )(q, k, v, z, g1, w_g2, b_g.reshape(1, GLA_KW), norm_g.reshape(1, D_INNER), mst, lvl)


PROJ_TM = 1376
PROJ_TN = 512
OPROJ_TM, OPROJ_TN, OPROJ_TK = 1376, 2048, 512
LN_TR0 = 192
LN_TR1 = 256
HG_HEADS_PER_STEP = 4


def kernel(x, meta, lb_logits, l0_w_in, l0_b_f, l0_norm_g, l0_w_out, l0_ln_g, l0_ln_b,
           l1_w_in, l1_w_g1, l1_w_g2, l1_b_g, l1_norm_g, l1_w_out, l1_ln_g, l1_ln_b):
    bsz = x.shape[0]
    m = bsz * T_TOT
    meta_b = jnp.broadcast_to(meta[None].astype(x.dtype), (bsz, N_META, D_MODEL))
    h0 = jnp.concatenate([x, meta_b], axis=1)
    h0_2d = h0.reshape(m, D_MODEL)
    a0 = h0_2d.astype(BF16)

    proj0 = functools.partial(_proj, a0, l0_w_in, tm=PROJ_TM, tn=PROJ_TN)
    q = proj0(0 * D_INNER, D_INNER, out_dtype=BF16, act="silu", name="l0_proj_q")
    f = proj0(1 * D_INNER, D_INNER, out_dtype=F32, name="l0_proj_f")
    v = proj0(2 * D_INNER, D_INNER, out_dtype=BF16, name="l0_proj_i")
    z = proj0(3 * D_INNER, D_INNER, out_dtype=BF16, act="silu", name="l0_proj_z")
    s3 = (bsz, T_TOT, D_INNER)
    y = _hgrn2_attention(q.reshape(s3), f.reshape(s3), v.reshape(s3), z.reshape(s3),
                         l0_b_f, lb_logits, l0_norm_g, heads=HG_HEADS_PER_STEP)
    br = _oproj(y.reshape(m, D_INNER), l0_w_out, tm=OPROJ_TM, tn=OPROJ_TN, tk=OPROJ_TK,
                name="l0_oproj")
    h1, a1 = _ln_all_rows(h0_2d, br, l0_ln_g, l0_ln_b, tr=LN_TR0, name="l0_ln")

    proj1 = functools.partial(_proj, a1, l1_w_in, tm=PROJ_TM, tn=PROJ_TN)
    q = proj1(0, GLA_KW, out_dtype=BF16, act="scale", scale=GLA_DK ** -0.5, name="l1_proj_q")
    k = proj1(GLA_KW, GLA_KW, out_dtype=BF16, name="l1_proj_k")
    v = proj1(2 * GLA_KW, D_INNER, out_dtype=BF16, name="l1_proj_v")
    z = proj1(2 * GLA_KW + D_INNER, D_INNER, out_dtype=BF16, act="silu", name="l1_proj_z")
    w_g1p = jnp.pad(l1_w_g1, ((0, 0), (0, LANE - GLA_RANK)))
    w_g2p = jnp.pad(l1_w_g2, ((0, LANE - GLA_RANK), (0, 0)))
    g1 = _proj(a1, w_g1p, 0, LANE, tm=PROJ_TM, tn=LANE, out_dtype=F32, name="l1_proj_g1")
    y = _gla_attention(q.reshape(bsz, T_TOT, GLA_KW), k.reshape(bsz, T_TOT, GLA_KW),
                       v.reshape(s3), z.reshape(s3), g1.reshape(bsz, T_TOT, LANE),
                       w_g2p, l1_b_g, l1_norm_g)
    br = _oproj(y.reshape(m, D_INNER), l1_w_out, tm=OPROJ_TM, tn=OPROJ_TN, tk=OPROJ_TK,
                name="l1_oproj")
    return _ln_seq_rows(h1.reshape(bsz, T_TOT, D_MODEL), br.reshape(bsz, T_TOT, D_MODEL),
                        l1_ln_g, l1_ln_b, tr=LN_TR1, name="l1_ln")
```

```python
import functools

import numpy as np
import jax
import jax.numpy as jnp
from jax import lax
from jax.experimental import pallas as pl
from jax.experimental.pallas import tpu as pltpu

D_MODEL = 4096
SEQ = 2048
N_META = 16
T_TOT = SEQ + N_META
D_INNER = 2 * D_MODEL
HG_DK = 128
GLA_HEADS = 8
GLA_KW = D_INNER // 2
GLA_DK = GLA_KW // GLA_HEADS
GLA_DV = D_INNER // GLA_HEADS
GLA_RANK = 16
GLA_GATE_NORM = 16.0
DEPTH = 2
ALPHA = (2.0 * DEPTH) ** 0.25
LN_EPS = 1e-5
RMS_EPS = 1e-6
LOG2_E = 1.4426950408889634

SUB = 16
BLK = 128
LEVELS = (32, 64, 128)
N_SUB = BLK // SUB
LANE = 128
VMEM_LIMIT_V7X = 60 * 1024 * 1024

F32 = jnp.float32
BF16 = jnp.bfloat16


def _dot(a, b):
    return jnp.dot(a, b, preferred_element_type=F32)


def _dot_nt(a, b):
    return lax.dot_general(a, b, (((1,), (1,)), ((), ())), preferred_element_type=F32)


def _dot_tn(a, b):
    return lax.dot_general(a, b, (((0,), (0,)), ((), ())), preferred_element_type=F32)


def _sigmoid(x):
    return 1.0 / (1.0 + jnp.exp(-x))


def _split_bf16(x):
    hi = x.astype(BF16)
    return hi, (x - hi.astype(F32)).astype(BF16)


def _ep_identity(acc):
    return (acc,)


def _ep_silu(acc):
    return (acc * _sigmoid(acc),)


def _ep_scale(acc, *, scale):
    return (acc * scale,)


def _ep_silu_gain(acc, gain):
    return (acc * _sigmoid(acc) * gain,)


def _ep_hgrn2_gate(acc, b_f, lb_logits):
    e = jnp.exp(lb_logits - jnp.max(lb_logits, axis=0, keepdims=True))
    lb = e[0:1, :] / jnp.sum(e, axis=0, keepdims=True)
    fg = lb + (1.0 - lb) * _sigmoid(acc + b_f)
    hi, lo = _split_bf16(jnp.log2(fg))
    return hi, lo, 1.0 - fg


def _ep_gla_gate(acc, b_g):
    u = acc + b_g
    log_sig = jnp.minimum(u, 0.0) - jnp.log(1.0 + jnp.exp(-jnp.abs(u)))
    return _split_bf16(log_sig * (LOG2_E / GLA_GATE_NORM))


def _proj_kernel(a_ref, w_ref, *refs, epilogue, n_vec):
    vecs = [r[...] for r in refs[:n_vec]]
    acc = _dot(a_ref[...], w_ref[...].astype(BF16))
    for o_ref, val in zip(refs[n_vec:], epilogue(acc, *vecs)):
        o_ref[...] = val.astype(o_ref.dtype)


def _proj(a, w, col0, ncols, *, tm, tn, out_dtypes, epilogue=_ep_identity, vecs=(), name):
    m, k = a.shape
    assert m % tm == 0 and ncols % tn == 0 and col0 % tn == 0
    jb = col0 // tn
    outs = pl.pallas_call(
        functools.partial(_proj_kernel, epilogue=epilogue, n_vec=len(vecs)),
        out_shape=tuple(jax.ShapeDtypeStruct((m, ncols), dt) for dt in out_dtypes),
        grid=(m // tm, ncols // tn),
        in_specs=[
            pl.BlockSpec((tm, k), lambda i, j: (i, 0)),
            pl.BlockSpec((k, tn), lambda i, j: (0, j + jb)),
        ] + [pl.BlockSpec((vv.shape[0], tn), lambda i, j: (0, j)) for vv in vecs],
        out_specs=tuple(pl.BlockSpec((tm, tn), lambda i, j: (i, j)) for _ in out_dtypes),
        compiler_params=pltpu.CompilerParams(
            dimension_semantics=("arbitrary", "arbitrary"),
            vmem_limit_bytes=VMEM_LIMIT_V7X),
        name=name,
    )(a, w, *vecs)
    return outs if len(outs) > 1 else outs[0]


def _oproj_kernel(a_ref, w_ref, o_ref, acc_ref):
    kk = pl.program_id(2)

    @pl.when(kk == 0)
    def _():
        acc_ref[...] = jnp.zeros_like(acc_ref)

    acc_ref[...] += _dot(a_ref[...], w_ref[...].astype(BF16))

    @pl.when(kk == pl.num_programs(2) - 1)
    def _():
        o_ref[...] = acc_ref[...].astype(o_ref.dtype)


def _oproj(a, w, *, tm, tn, tk, name):
    m, k = a.shape
    n = w.shape[1]
    assert m % tm == 0 and n % tn == 0 and k % tk == 0
    return pl.pallas_call(
        _oproj_kernel,
        out_shape=jax.ShapeDtypeStruct((m, n), BF16),
        grid=(m // tm, n // tn, k // tk),
        in_specs=[
            pl.BlockSpec((tm, tk), lambda i, j, kk: (i, kk)),
            pl.BlockSpec((tk, tn), lambda i, j, kk: (kk, j)),
        ],
        out_specs=pl.BlockSpec((tm, tn), lambda i, j, kk: (i, j)),
        scratch_shapes=[pltpu.VMEM((tm, tn), F32)],
        compiler_params=pltpu.CompilerParams(
            dimension_semantics=("arbitrary", "arbitrary", "arbitrary"),
            vmem_limit_bytes=VMEM_LIMIT_V7X),
        name=name,
    )(a, w)


def _ln_kernel(h_ref, br_ref, g_ref, b_ref, *o_refs):
    r = ALPHA * h_ref[...] + br_ref[...].astype(F32)
    mu = jnp.mean(r, axis=-1, keepdims=True)
    c = r - mu
    var = jnp.mean(c * c, axis=-1, keepdims=True)
    y = c * lax.rsqrt(var + LN_EPS) * g_ref[...] + b_ref[...]
    for o_ref in o_refs:
        o_ref[...] = y.astype(o_ref.dtype)


def _ln_all_rows(h, br, g, b, *, tr, name):
    m, d = h.shape
    assert m % tr == 0
    row = pl.BlockSpec((tr, d), lambda i: (i, 0))
    vec = pl.BlockSpec((1, d), lambda i: (0, 0))
    return pl.pallas_call(
        _ln_kernel,
        out_shape=(jax.ShapeDtypeStruct((m, d), F32), jax.ShapeDtypeStruct((m, d), BF16)),
        grid=(m // tr,),
        in_specs=[row, row, vec, vec],
        out_specs=(row, row),
        compiler_params=pltpu.CompilerParams(
            dimension_semantics=("arbitrary",), vmem_limit_bytes=VMEM_LIMIT_V7X),
        name=name,
    )(h, br, g.reshape(1, d), b.reshape(1, d))


def _ln_seq_rows(h, br, g, b, *, tr, name):
    bsz, _, d = h.shape
    assert SEQ % tr == 0
    row = pl.BlockSpec((None, tr, d), lambda bi, i: (bi, i, 0))
    vec = pl.BlockSpec((1, d), lambda bi, i: (0, 0))
    return pl.pallas_call(
        _ln_kernel,
        out_shape=jax.ShapeDtypeStruct((bsz, SEQ, d), F32),
        grid=(bsz, SEQ // tr),
        in_specs=[row, row, vec, vec],
        out_specs=row,
        compiler_params=pltpu.CompilerParams(
            dimension_semantics=("arbitrary", "arbitrary"), vmem_limit_bytes=VMEM_LIMIT_V7X),
        name=name,
    )(h, br, g.reshape(1, d), b.reshape(1, d))


def _decay_consts():
    c = BLK
    i = np.arange(c)[:, None]
    s = np.arange(c)[None, :]
    tri = (s <= i).astype(np.float32)
    lvl = np.full((c, c), -1, np.int32)
    lvl[((i // SUB) == (s // SUB)) & (s <= i)] = 0
    for l, g in enumerate(LEVELS):
        half = g // 2
        lvl[((i // g) == (s // g)) & ((i % g) >= half) & ((s % g) < half)] = l + 1
    return jnp.asarray(tri, BF16), jnp.asarray(lvl)


def _is_upper(p, g):
    return (p * SUB) % g >= g // 2


def _pieces(x):
    return [x[p * SUB:(p + 1) * SUB, :] for p in range(N_SUB)]


def _rows(b, r, n):
    return jnp.broadcast_to(b[r:r + 1, :], (n, b.shape[1]))


def _decay_exponents(b):
    c = BLK
    e0 = b - jnp.concatenate(
        [jnp.zeros((SUB, b.shape[1]), F32)]
        + [_rows(b, s * SUB - 1, SUB) for s in range(1, N_SUB)], axis=0)
    levels = []
    for g in LEVELS:
        half = g // 2
        parts = []
        for s in range(c // g):
            ref = _rows(b, s * g + half - 1, half)
            parts.append(ref - b[s * g:s * g + half, :])
            parts.append(b[s * g + half:(s + 1) * g, :] - ref)
        levels.append(jnp.concatenate(parts, axis=0))
    e_k = _rows(b, c - 1, c) - b
    return e0, levels, e_k


def _scaled_operands(q, k, b):
    e0, e_lv, e_k = _decay_exponents(b)
    ops = {"q0": q * jnp.exp2(e0).astype(BF16), "k0": k * jnp.exp2(-e0).astype(BF16),
           "q_up": [], "k_mix": []}
    q_pc, k_pc = _pieces(q), _pieces(k)
    for g, e_l in zip(LEVELS, e_lv):
        x_pc = _pieces(jnp.exp2(e_l).astype(BF16))
        ops["q_up"].append(jnp.concatenate(
            [q_pc[p] * x_pc[p] for p in range(N_SUB) if _is_upper(p, g)], axis=0))
        ops["k_mix"].append(jnp.concatenate(
            [k_pc[p] if _is_upper(p, g) else k_pc[p] * x_pc[p] for p in range(N_SUB)], axis=0))
    ops["q_dec"] = q * jnp.exp2(b).astype(BF16)
    ops["k_dec"] = k * jnp.exp2(e_k).astype(BF16)
    ops["g_last"] = jnp.exp2(b[BLK - 1:BLK, :])
    return ops


def _masked_scores(raw, lvl, lvl_pc):
    sc_pc = _pieces(jnp.where(lvl == 0, raw[0], 0.0))
    for l, g in enumerate(LEVELS):
        ups = [p for p in range(N_SUB) if _is_upper(p, g)]
        for j, p in enumerate(ups):
            sc_pc[p] = jnp.where(lvl_pc[p] == l + 1, raw[l + 1][j * SUB:(j + 1) * SUB, :], sc_pc[p])
    return jnp.concatenate(sc_pc, axis=0).astype(BF16)


def _norm_gate(o, zg):
    ms = jnp.mean(o * o, axis=-1, keepdims=True)
    return o * lax.rsqrt(ms + RMS_EPS) * zg


META_WIN0 = T_TOT - BLK
META_OFF = BLK - N_META


def _attn_kernel(q_ref, k_ref, lh_ref, ll_ref, v_ref, zg_ref, tri_ref, lvl_ref, y_ref, st_ref,
                 *, heads, per):
    lvl = lvl_ref[...]
    lvl_pc = _pieces(lvl)
    tri = tri_ref[...]
    dk = q_ref.shape[1] // heads
    dv = v_ref.shape[1] // heads
    valid = lax.broadcasted_iota(jnp.int32, (BLK, 1), 0) >= META_OFF
    hs = range(heads)
    ksl = [slice(h * dk, (h + 1) * dk) for h in hs]
    vsl = [slice(h * dv, (h + 1) * dv) for h in hs]

    def blocks(starts, first):
        nb = range(len(starts))
        rows = [pl.ds(r0, BLK) for r0 in starts]
        lg = [(lh_ref[r, :], ll_ref[r, :]) for r in rows]
        if first:
            lg = [tuple(jnp.where(valid, t, jnp.zeros_like(t)) for t in pr) for pr in lg]
        cum = [_dot(tri, hi) + _dot(tri, lo) for hi, lo in lg]
        ops, vals = [], []
        for n in nb:
            k, v = k_ref[rows[n], :], v_ref[rows[n], :]
            if first:
                k, v = (jnp.where(valid, t, jnp.zeros_like(t)) for t in (k, v))
            ops.append(_scaled_operands(q_ref[rows[n], :], k, cum[n]))
            vals.append(v)
        raw = [[[_dot_nt(ops[n]["q0"][:, ksl[h]], ops[n]["k0"][:, ksl[h]])]
                + [_dot_nt(ops[n]["q_up"][l][:, ksl[h]], ops[n]["k_mix"][l][:, ksl[h]])
                   for l in range(len(LEVELS))] for h in hs] for n in nb]
        sc = [[_masked_scores(raw[n][h], lvl, lvl_pc) for h in hs] for n in nb]
        o = [[_dot(sc[n][h], vals[n][:, vsl[h]]) for h in hs] for n in nb]
        upd = [[_dot_tn(vals[n][:, vsl[h]], ops[n]["k_dec"][:, ksl[h]]) for h in hs] for n in nb]
        for n in nb:
            for h in hs:
                if first:
                    st_ref[h] = upd[n][h]
                else:
                    st = st_ref[h]
                    o[n][h] = o[n][h] + _dot_nt(ops[n]["q_dec"][:, ksl[h]], st.astype(BF16))
                    st_ref[h] = ops[n]["g_last"][:, ksl[h]] * st + upd[n][h]
        for n in nb:
            for h in hs:
                y = _norm_gate(o[n][h], zg_ref[rows[n], vsl[h]].astype(F32)).astype(y_ref.dtype)
                if first:
                    y_ref[pl.ds(SEQ, N_META), vsl[h]] = y[META_OFF:, :]
                else:
                    y_ref[rows[n], vsl[h]] = y

    blocks([META_WIN0], True)

    def step(i, carry):
        base = pl.multiple_of(i * (per * BLK), per * BLK)
        blocks([base + n * BLK for n in range(per)], False)
        return carry

    lax.fori_loop(0, SEQ // (per * BLK), step, 0)


def _attention(q, k, lg_hi, lg_lo, v, zg, *, heads, per, dk, dv, name):
    assert SEQ % (per * BLK) == 0
    bsz, t, _ = q.shape
    n_groups = v.shape[2] // (heads * dv)
    tri, lvl = _decay_consts()
    qk = pl.BlockSpec((None, t, heads * dk), lambda bi, g: (bi, 0, g))
    vz = pl.BlockSpec((None, t, heads * dv), lambda bi, g: (bi, 0, g))
    return pl.pallas_call(
        functools.partial(_attn_kernel, heads=heads, per=per),
        out_shape=jax.ShapeDtypeStruct(v.shape, BF16),
        grid=(bsz, n_groups),
        in_specs=[qk, qk, qk, qk, vz, vz,
                  pl.BlockSpec(tri.shape, lambda bi, g: (0, 0)),
                  pl.BlockSpec(lvl.shape, lambda bi, g: (0, 0))],
        out_specs=vz,
        scratch_shapes=[pltpu.VMEM((heads, dv, dk), F32)],
        compiler_params=pltpu.CompilerParams(
            dimension_semantics=("arbitrary", "arbitrary"), vmem_limit_bytes=VMEM_LIMIT_V7X),
        name=name,
    )(q, k, lg_hi, lg_lo, v, zg, tri, lvl)


PROJ_TM = 1376
PROJ_TN = 512
OPROJ_TM, OPROJ_TN, OPROJ_TK = 1376, 2048, 512
LN_TR0 = 192
LN_TR1 = 256
HG_HEADS_PER_STEP = 4


def kernel(x, meta, lb_logits, l0_w_in, l0_b_f, l0_norm_g, l0_w_out, l0_ln_g, l0_ln_b,
           l1_w_in, l1_w_g1, l1_w_g2, l1_b_g, l1_norm_g, l1_w_out, l1_ln_g, l1_ln_b):
    bsz = x.shape[0]
    m = bsz * T_TOT
    meta_b = jnp.broadcast_to(meta[None].astype(x.dtype), (bsz, N_META, D_MODEL))
    h0 = jnp.concatenate([x, meta_b], axis=1)
    h0_2d = h0.reshape(m, D_MODEL)
    a0 = h0_2d.astype(BF16)
    s3 = (bsz, T_TOT, D_INNER)
    row = lambda vv: vv.reshape(1, -1)

    proj0 = functools.partial(_proj, a0, l0_w_in, tm=PROJ_TM, tn=PROJ_TN)
    q = proj0(0 * D_INNER, D_INNER, out_dtypes=(BF16,), epilogue=_ep_silu, name="l0_proj_q")
    lg_hi, lg_lo, k = proj0(1 * D_INNER, D_INNER, out_dtypes=(BF16, BF16, BF16),
                            epilogue=_ep_hgrn2_gate, vecs=(row(l0_b_f), lb_logits),
                            name="l0_proj_f")
    v = proj0(2 * D_INNER, D_INNER, out_dtypes=(BF16,), name="l0_proj_i")
    zg = proj0(3 * D_INNER, D_INNER, out_dtypes=(BF16,), epilogue=_ep_silu_gain,
               vecs=(row(l0_norm_g),), name="l0_proj_z")
    y = _attention(*(t.reshape(s3) for t in (q, k, lg_hi, lg_lo, v, zg)),
                   heads=HG_HEADS_PER_STEP, per=4, dk=HG_DK, dv=HG_DK, name="hgrn2_attention")
    br = _oproj(y.reshape(m, D_INNER), l0_w_out, tm=OPROJ_TM, tn=OPROJ_TN, tk=OPROJ_TK,
                name="l0_oproj")
    h1, a1 = _ln_all_rows(h0_2d, br, l0_ln_g, l0_ln_b, tr=LN_TR0, name="l0_ln")

    proj1 = functools.partial(_proj, a1, l1_w_in, tm=PROJ_TM, tn=PROJ_TN)
    q = proj1(0, GLA_KW, out_dtypes=(BF16,),
              epilogue=functools.partial(_ep_scale, scale=GLA_DK ** -0.5), name="l1_proj_q")
    k = proj1(GLA_KW, GLA_KW, out_dtypes=(BF16,), name="l1_proj_k")
    v = proj1(2 * GLA_KW, D_INNER, out_dtypes=(BF16,), name="l1_proj_v")
    zg = proj1(2 * GLA_KW + D_INNER, D_INNER, out_dtypes=(BF16,), epilogue=_ep_silu_gain,
               vecs=(row(l1_norm_g),), name="l1_proj_z")
    w_g1p = jnp.pad(l1_w_g1, ((0, 0), (0, LANE - GLA_RANK)))
    w_g2p = jnp.pad(l1_w_g2, ((0, LANE - GLA_RANK), (0, 0)))
    g1 = _proj(a1, w_g1p, 0, LANE, tm=PROJ_TM, tn=LANE, out_dtypes=(BF16,), name="l1_proj_g1")
    lg_hi, lg_lo = _proj(g1, w_g2p, 0, GLA_KW, tm=PROJ_TM, tn=PROJ_TN, out_dtypes=(BF16, BF16),
                         epilogue=_ep_gla_gate, vecs=(row(l1_b_g),), name="l1_gate")
    s2 = (bsz, T_TOT, GLA_KW)
    y = _attention(q.reshape(s2), k.reshape(s2), lg_hi.reshape(s2), lg_lo.reshape(s2),
                   v.reshape(s3), zg.reshape(s3),
                   heads=1, per=2, dk=GLA_DK, dv=GLA_DV, name="gla_attention")
    br = _oproj(y.reshape(m, D_INNER), l1_w_out, tm=OPROJ_TM, tn=OPROJ_TN, tk=OPROJ_TK,
                name="l1_oproj")
    return _ln_seq_rows(h1.reshape(bsz, T_TOT, D_MODEL), br.reshape(bsz, T_TOT, D_MODEL),
                        l1_ln_g, l1_ln_b, tr=LN_TR1, name="l1_ln")
```

```python
import functools

import numpy as np
import jax
import jax.numpy as jnp
from jax import lax
from jax.experimental import pallas as pl
from jax.experimental.pallas import tpu as pltpu

D_MODEL = 4096
SEQ = 2048
N_META = 16
T_TOT = SEQ + N_META
D_INNER = 2 * D_MODEL
HG_DK = 128
GLA_HEADS = 8
GLA_KW = D_INNER // 2
GLA_DK = GLA_KW // GLA_HEADS
GLA_DV = D_INNER // GLA_HEADS
GLA_RANK = 16
GLA_GATE_NORM = 16.0
DEPTH = 2
ALPHA = (2.0 * DEPTH) ** 0.25
LN_EPS = 1e-5
RMS_EPS = 1e-6
LOG2_E = 1.4426950408889634

SUB = 16
BLK = 128
LEVELS = (32, 64, 128)
N_SUB = BLK // SUB
LANE = 128
VMEM_LIMIT_V7X = 60 * 1024 * 1024

F32 = jnp.float32
BF16 = jnp.bfloat16


def _dot(a, b):
    return jnp.dot(a, b, preferred_element_type=F32)


def _dot_nt(a, b):
    return lax.dot_general(a, b, (((1,), (1,)), ((), ())), preferred_element_type=F32)


def _dot_tn(a, b):
    return lax.dot_general(a, b, (((0,), (0,)), ((), ())), preferred_element_type=F32)


def _sigmoid(x):
    return 1.0 / (1.0 + jnp.exp(-x))


def _split_bf16(x):
    hi = x.astype(BF16)
    return hi, (x - hi.astype(F32)).astype(BF16)


def _ep_identity(acc):
    return (acc,)


def _ep_silu(acc):
    return (acc * _sigmoid(acc),)


def _ep_scale(acc, *, scale):
    return (acc * scale,)


def _ep_silu_gain(acc, gain):
    return (acc * _sigmoid(acc) * gain,)


def _ep_hgrn2_gate(acc, b_f, lb_logits):
    e = jnp.exp(lb_logits - jnp.max(lb_logits, axis=0, keepdims=True))
    lb = e[0:1, :] / jnp.sum(e, axis=0, keepdims=True)
    fg = lb + (1.0 - lb) * _sigmoid(acc + b_f)
    hi, lo = _split_bf16(jnp.log2(fg))
    return hi, lo, 1.0 - fg


def _ep_gla_gate(acc, b_g):
    u = acc + b_g
    log_sig = jnp.minimum(u, 0.0) - jnp.log(1.0 + jnp.exp(-jnp.abs(u)))
    return _split_bf16(log_sig * (LOG2_E / GLA_GATE_NORM))


def _proj_kernel(a_ref, w_ref, *refs, epilogue, n_vec):
    vecs = [r[...] for r in refs[:n_vec]]
    acc = _dot(a_ref[...], w_ref[...].astype(BF16))
    for o_ref, val in zip(refs[n_vec:], epilogue(acc, *vecs)):
        o_ref[...] = val.astype(o_ref.dtype)


def _proj(a, w, col0, ncols, *, tm, tn, out_dtypes, epilogue=_ep_identity, vecs=(), name):
    m, k = a.shape
    assert m % tm == 0 and ncols % tn == 0 and col0 % tn == 0
    jb = col0 // tn
    outs = pl.pallas_call(
        functools.partial(_proj_kernel, epilogue=epilogue, n_vec=len(vecs)),
        out_shape=tuple(jax.ShapeDtypeStruct((m, ncols), dt) for dt in out_dtypes),
        grid=(m // tm, ncols // tn),
        in_specs=[
            pl.BlockSpec((tm, k), lambda i, j: (i, 0)),
            pl.BlockSpec((k, tn), lambda i, j: (0, j + jb)),
        ] + [pl.BlockSpec((vv.shape[0], tn), lambda i, j: (0, j)) for vv in vecs],
        out_specs=tuple(pl.BlockSpec((tm, tn), lambda i, j: (i, j)) for _ in out_dtypes),
        compiler_params=pltpu.CompilerParams(
            dimension_semantics=("arbitrary", "arbitrary"),
            vmem_limit_bytes=VMEM_LIMIT_V7X),
        name=name,
    )(a, w, *vecs)
    return outs if len(outs) > 1 else outs[0]


def _oproj_kernel(a_ref, w_ref, o_ref, acc_ref):
    kk = pl.program_id(2)

    @pl.when(kk == 0)
    def _():
        acc_ref[...] = jnp.zeros_like(acc_ref)

    acc_ref[...] += _dot(a_ref[...], w_ref[...].astype(BF16))

    @pl.when(kk == pl.num_programs(2) - 1)
    def _():
        o_ref[...] = acc_ref[...].astype(o_ref.dtype)


def _oproj(a, w, *, tm, tn, tk, name):
    m, k = a.shape
    n = w.shape[1]
    assert m % tm == 0 and n % tn == 0 and k % tk == 0
    return pl.pallas_call(
        _oproj_kernel,
        out_shape=jax.ShapeDtypeStruct((m, n), BF16),
        grid=(m // tm, n // tn, k // tk),
        in_specs=[
            pl.BlockSpec((tm, tk), lambda i, j, kk: (i, kk)),
            pl.BlockSpec((tk, tn), lambda i, j, kk: (kk, j)),
        ],
        out_specs=pl.BlockSpec((tm, tn), lambda i, j, kk: (i, j)),
        scratch_shapes=[pltpu.VMEM((tm, tn), F32)],
        compiler_params=pltpu.CompilerParams(
            dimension_semantics=("arbitrary", "arbitrary", "arbitrary"),
            vmem_limit_bytes=VMEM_LIMIT_V7X),
        name=name,
    )(a, w)


def _concat_cast_kernel(x_ref, meta_ref, o_ref):
    i = pl.program_id(1)

    @pl.when(i < pl.num_programs(1) - 1)
    def _():
        o_ref[...] = x_ref[...].astype(o_ref.dtype)

    @pl.when(i == pl.num_programs(1) - 1)
    def _():
        o_ref[0:N_META, :] = meta_ref[...].astype(o_ref.dtype)


def _concat_cast(x, meta, *, tr):
    bsz, _, d = x.shape
    n_x = SEQ // tr
    return pl.pallas_call(
        _concat_cast_kernel,
        out_shape=jax.ShapeDtypeStruct((bsz, T_TOT, d), BF16),
        grid=(bsz, n_x + 1),
        in_specs=[pl.BlockSpec((None, tr, d), lambda bi, i: (bi, jnp.minimum(i, n_x - 1), 0)),
                  pl.BlockSpec((N_META, d), lambda bi, i: (0, 0))],
        out_specs=pl.BlockSpec((None, tr, d), lambda bi, i: (bi, i, 0)),
        compiler_params=pltpu.CompilerParams(
            dimension_semantics=("arbitrary", "arbitrary"), vmem_limit_bytes=VMEM_LIMIT_V7X),
        name="concat_cast",
    )(x, meta)


def _ln_rows(h, br, g, b):
    r = ALPHA * h + br.astype(F32)
    mu = jnp.mean(r, axis=-1, keepdims=True)
    c = r - mu
    var = jnp.mean(c * c, axis=-1, keepdims=True)
    return c * lax.rsqrt(var + LN_EPS) * g + b


def _ln_first_kernel(x_ref, meta_ref, br_ref, g_ref, b_ref, o32_ref, o16_ref):
    i = pl.program_id(1)

    @pl.when(i < pl.num_programs(1) - 1)
    def _():
        y = _ln_rows(x_ref[...], br_ref[...], g_ref[...], b_ref[...])
        o32_ref[...] = y
        o16_ref[...] = y.astype(o16_ref.dtype)

    @pl.when(i == pl.num_programs(1) - 1)
    def _():
        y = _ln_rows(meta_ref[...], br_ref[0:N_META, :], g_ref[...], b_ref[...])
        o32_ref[0:N_META, :] = y
        o16_ref[0:N_META, :] = y.astype(o16_ref.dtype)


def _ln_first(x, meta, br, g, b, *, tr):
    bsz, _, d = x.shape
    n_x = SEQ // tr
    blk = pl.BlockSpec((None, tr, d), lambda bi, i: (bi, i, 0))
    vec = pl.BlockSpec((1, d), lambda bi, i: (0, 0))
    return pl.pallas_call(
        _ln_first_kernel,
        out_shape=(jax.ShapeDtypeStruct((bsz, T_TOT, d), F32),
                   jax.ShapeDtypeStruct((bsz, T_TOT, d), BF16)),
        grid=(bsz, n_x + 1),
        in_specs=[pl.BlockSpec((None, tr, d), lambda bi, i: (bi, jnp.minimum(i, n_x - 1), 0)),
                  pl.BlockSpec((N_META, d), lambda bi, i: (0, 0)), blk, vec, vec],
        out_specs=(blk, blk),
        compiler_params=pltpu.CompilerParams(
            dimension_semantics=("arbitrary", "arbitrary"), vmem_limit_bytes=VMEM_LIMIT_V7X),
        name="l0_ln",
    )(x, meta, br, g.reshape(1, d), b.reshape(1, d))


def _ln_kernel(h_ref, br_ref, g_ref, b_ref, o_ref):
    o_ref[...] = _ln_rows(h_ref[...], br_ref[...], g_ref[...], b_ref[...])


def _ln_seq_rows(h, br, g, b, *, tr, name):
    bsz, _, d = h.shape
    assert SEQ % tr == 0
    row = pl.BlockSpec((None, tr, d), lambda bi, i: (bi, i, 0))
    vec = pl.BlockSpec((1, d), lambda bi, i: (0, 0))
    return pl.pallas_call(
        _ln_kernel,
        out_shape=jax.ShapeDtypeStruct((bsz, SEQ, d), F32),
        grid=(bsz, SEQ // tr),
        in_specs=[row, row, vec, vec],
        out_specs=row,
        compiler_params=pltpu.CompilerParams(
            dimension_semantics=("arbitrary", "arbitrary"), vmem_limit_bytes=VMEM_LIMIT_V7X),
        name=name,
    )(h, br, g.reshape(1, d), b.reshape(1, d))


def _decay_consts():
    c = BLK
    i = np.arange(c)[:, None]
    s = np.arange(c)[None, :]
    tri = (s <= i).astype(np.float32)
    lvl = np.full((c, c), -1, np.int32)
    lvl[((i // SUB) == (s // SUB)) & (s <= i)] = 0
    for l, g in enumerate(LEVELS):
        half = g // 2
        lvl[((i // g) == (s // g)) & ((i % g) >= half) & ((s % g) < half)] = l + 1
    return jnp.asarray(tri, BF16), jnp.asarray(lvl)


def _is_upper(p, g):
    return (p * SUB) % g >= g // 2


def _pieces(x):
    return [x[p * SUB:(p + 1) * SUB, :] for p in range(N_SUB)]


def _rows(b, r, n):
    return jnp.broadcast_to(b[r:r + 1, :], (n, b.shape[1]))


def _decay_exponents(b):
    c = BLK
    e0 = b - jnp.concatenate(
        [jnp.zeros((SUB, b.shape[1]), F32)]
        + [_rows(b, s * SUB - 1, SUB) for s in range(1, N_SUB)], axis=0)
    levels = []
    for g in LEVELS:
        half = g // 2
        parts = []
        for s in range(c // g):
            ref = _rows(b, s * g + half - 1, half)
            parts.append(ref - b[s * g:s * g + half, :])
            parts.append(b[s * g + half:(s + 1) * g, :] - ref)
        levels.append(jnp.concatenate(parts, axis=0))
    e_k = _rows(b, c - 1, c) - b
    return e0, levels, e_k


def _scaled_operands(q, k, b):
    e0, e_lv, e_k = _decay_exponents(b)
    ops = {"q0": q * jnp.exp2(e0).astype(BF16), "k0": k * jnp.exp2(-e0).astype(BF16),
           "q_up": [], "k_mix": []}
    q_pc, k_pc = _pieces(q), _pieces(k)
    for g, e_l in zip(LEVELS, e_lv):
        x_pc = _pieces(jnp.exp2(e_l).astype(BF16))
        ops["q_up"].append(jnp.concatenate(
            [q_pc[p] * x_pc[p] for p in range(N_SUB) if _is_upper(p, g)], axis=0))
        ops["k_mix"].append(jnp.concatenate(
            [k_pc[p] if _is_upper(p, g) else k_pc[p] * x_pc[p] for p in range(N_SUB)], axis=0))
    ops["q_dec"] = q * jnp.exp2(b).astype(BF16)
    ops["k_dec"] = k * jnp.exp2(e_k).astype(BF16)
    ops["g_last"] = jnp.exp2(b[BLK - 1:BLK, :])
    return ops


def _masked_scores(raw, lvl, lvl_pc):
    sc_pc = _pieces(jnp.where(lvl == 0, raw[0], 0.0))
    for l, g in enumerate(LEVELS):
        ups = [p for p in range(N_SUB) if _is_upper(p, g)]
        for j, p in enumerate(ups):
            sc_pc[p] = jnp.where(lvl_pc[p] == l + 1, raw[l + 1][j * SUB:(j + 1) * SUB, :], sc_pc[p])
    return jnp.concatenate(sc_pc, axis=0).astype(BF16)


def _norm_gate(o, zg):
    ms = jnp.mean(o * o, axis=-1, keepdims=True)
    return o * lax.rsqrt(ms + RMS_EPS) * zg


def _attn_kernel(q_ref, k_ref, lh_ref, ll_ref, v_ref, zg_ref, tri_ref, lvl_ref, y_ref, st_ref,
                 *, heads, per):
    lvl = lvl_ref[...]
    lvl_pc = _pieces(lvl)
    tri = tri_ref[...]
    dk = q_ref.shape[1] // heads
    dv = v_ref.shape[1] // heads
    hs = range(heads)
    ksl = [slice(h * dk, (h + 1) * dk) for h in hs]
    vsl = [slice(h * dv, (h + 1) * dv) for h in hs]

    def pad_rows(t):
        return jnp.concatenate([t, jnp.zeros((BLK - N_META, t.shape[1]), t.dtype)], axis=0)

    def meta_block():
        rows = pl.ds(SEQ, N_META)
        b = (_dot(tri[0:N_META, :], pad_rows(lh_ref[rows, :]))
             + _dot(tri[0:N_META, :], pad_rows(ll_ref[rows, :])))
        q, k = q_ref[rows, :], k_ref[rows, :]
        q0 = q * jnp.exp2(b).astype(BF16)
        k0 = pad_rows(k * jnp.exp2(-b).astype(BF16))
        k_dec = pad_rows(k * jnp.exp2(b[N_META - 1:N_META, :] - b).astype(BF16))
        v = pad_rows(v_ref[rows, :])
        for h in hs:
            sc = jnp.where(lvl[0:N_META, :] == 0, _dot_nt(q0[:, ksl[h]], k0[:, ksl[h]]), 0.0)
            o = _dot(sc.astype(BF16), v[:, vsl[h]])
            st_ref[h] = _dot_tn(v[:, vsl[h]], k_dec[:, ksl[h]])
            y_ref[rows, vsl[h]] = _norm_gate(o, zg_ref[rows, vsl[h]].astype(F32)).astype(y_ref.dtype)

    def blocks(starts):
        nb = range(len(starts))
        rows = [pl.ds(r0, BLK) for r0 in starts]
        cum = [_dot(tri, lh_ref[r, :]) + _dot(tri, ll_ref[r, :]) for r in rows]
        ops = [_scaled_operands(q_ref[rows[n], :], k_ref[rows[n], :], cum[n]) for n in nb]
        vals = [v_ref[r, :] for r in rows]
        raw = [[[_dot_nt(ops[n]["q0"][:, ksl[h]], ops[n]["k0"][:, ksl[h]])]
                + [_dot_nt(ops[n]["q_up"][l][:, ksl[h]], ops[n]["k_mix"][l][:, ksl[h]])
                   for l in range(len(LEVELS))] for h in hs] for n in nb]
        sc = [[_masked_scores(raw[n][h], lvl, lvl_pc) for h in hs] for n in nb]
        o = [[_dot(sc[n][h], vals[n][:, vsl[h]]) for h in hs] for n in nb]
        upd = [[_dot_tn(vals[n][:, vsl[h]], ops[n]["k_dec"][:, ksl[h]]) for h in hs] for n in nb]
        for n in nb:
            for h in hs:
                st = st_ref[h]
                o[n][h] = o[n][h] + _dot_nt(ops[n]["q_dec"][:, ksl[h]], st.astype(BF16))
                st_ref[h] = ops[n]["g_last"][:, ksl[h]] * st + upd[n][h]
        for n in nb:
            for h in hs:
                y = _norm_gate(o[n][h], zg_ref[rows[n], vsl[h]].astype(F32))
                y_ref[rows[n], vsl[h]] = y.astype(y_ref.dtype)

    meta_block()

    def step(i, carry):
        base = pl.multiple_of(i * (per * BLK), per * BLK)
        blocks([base + n * BLK for n in range(per)])
        return carry

    lax.fori_loop(0, SEQ // (per * BLK), step, 0)


def _attention(q, k, lg_hi, lg_lo, v, zg, *, heads, per, dk, dv, name):
    assert SEQ % (per * BLK) == 0
    bsz, t, _ = q.shape
    n_groups = v.shape[2] // (heads * dv)
    tri, lvl = _decay_consts()
    qk = pl.BlockSpec((None, t, heads * dk), lambda bi, g: (bi, 0, g))
    vz = pl.BlockSpec((None, t, heads * dv), lambda bi, g: (bi, 0, g))
    return pl.pallas_call(
        functools.partial(_attn_kernel, heads=heads, per=per),
        out_shape=jax.ShapeDtypeStruct(v.shape, BF16),
        grid=(bsz, n_groups),
        in_specs=[qk, qk, qk, qk, vz, vz,
                  pl.BlockSpec(tri.shape, lambda bi, g: (0, 0)),
                  pl.BlockSpec(lvl.shape, lambda bi, g: (0, 0))],
        out_specs=vz,
        scratch_shapes=[pltpu.VMEM((heads, dv, dk), F32)],
        compiler_params=pltpu.CompilerParams(
            dimension_semantics=("arbitrary", "arbitrary"), vmem_limit_bytes=VMEM_LIMIT_V7X),
        name=name,
    )(q, k, lg_hi, lg_lo, v, zg, tri, lvl)


PROJ_TM = 1376
PROJ_TN = 512
OPROJ_TM, OPROJ_TN, OPROJ_TK = 1376, 1024, 1024
LN_TR = 256
HG_HEADS_PER_STEP = 4


def kernel(x, meta, lb_logits, l0_w_in, l0_b_f, l0_norm_g, l0_w_out, l0_ln_g, l0_ln_b,
           l1_w_in, l1_w_g1, l1_w_g2, l1_b_g, l1_norm_g, l1_w_out, l1_ln_g, l1_ln_b):
    bsz = x.shape[0]
    m = bsz * T_TOT
    a0 = _concat_cast(x, meta, tr=LN_TR).reshape(m, D_MODEL)
    s3 = (bsz, T_TOT, D_INNER)
    row = lambda vv: vv.reshape(1, -1)

    proj0 = functools.partial(_proj, a0, l0_w_in, tm=PROJ_TM, tn=PROJ_TN)
    q = proj0(0 * D_INNER, D_INNER, out_dtypes=(BF16,), epilogue=_ep_silu, name="l0_proj_q")
    lg_hi, lg_lo, k = proj0(1 * D_INNER, D_INNER, out_dtypes=(BF16, BF16, BF16),
                            epilogue=_ep_hgrn2_gate, vecs=(row(l0_b_f), lb_logits),
                            name="l0_proj_f")
    v = proj0(2 * D_INNER, D_INNER, out_dtypes=(BF16,), name="l0_proj_i")
    zg = proj0(3 * D_INNER, D_INNER, out_dtypes=(BF16,), epilogue=_ep_silu_gain,
               vecs=(row(l0_norm_g),), name="l0_proj_z")
    y = _attention(*(t.reshape(s3) for t in (q, k, lg_hi, lg_lo, v, zg)),
                   heads=HG_HEADS_PER_STEP, per=4, dk=HG_DK, dv=HG_DK, name="hgrn2_attention")
    br = _oproj(y.reshape(m, D_INNER), l0_w_out, tm=OPROJ_TM, tn=OPROJ_TN, tk=OPROJ_TK,
                name="l0_oproj")
    h1, a1 = _ln_first(x, meta, br.reshape(bsz, T_TOT, D_MODEL), l0_ln_g, l0_ln_b, tr=LN_TR)
    a1 = a1.reshape(m, D_MODEL)

    proj1 = functools.partial(_proj, a1, l1_w_in, tm=PROJ_TM, tn=PROJ_TN)
    q = proj1(0, GLA_KW, out_dtypes=(BF16,),
              epilogue=functools.partial(_ep_scale, scale=GLA_DK ** -0.5), name="l1_proj_q")
    k = proj1(GLA_KW, GLA_KW, out_dtypes=(BF16,), name="l1_proj_k")
    v = proj1(2 * GLA_KW, D_INNER, out_dtypes=(BF16,), name="l1_proj_v")
    zg = proj1(2 * GLA_KW + D_INNER, D_INNER, out_dtypes=(BF16,), epilogue=_ep_silu_gain,
               vecs=(row(l1_norm_g),), name="l1_proj_z")
    w_g1p = jnp.pad(l1_w_g1, ((0, 0), (0, LANE - GLA_RANK)))
    w_g2p = jnp.pad(l1_w_g2, ((0, LANE - GLA_RANK), (0, 0)))
    g1 = _proj(a1, w_g1p, 0, LANE, tm=PROJ_TM, tn=LANE, out_dtypes=(BF16,), name="l1_proj_g1")
    lg_hi, lg_lo = _proj(g1, w_g2p, 0, GLA_KW, tm=PROJ_TM, tn=PROJ_TN, out_dtypes=(BF16, BF16),
                         epilogue=_ep_gla_gate, vecs=(row(l1_b_g),), name="l1_gate")
    s2 = (bsz, T_TOT, GLA_KW)
    y = _attention(q.reshape(s2), k.reshape(s2), lg_hi.reshape(s2), lg_lo.reshape(s2),
                   v.reshape(s3), zg.reshape(s3),
                   heads=1, per=2, dk=GLA_DK, dv=GLA_DV, name="gla_attention")
    br = _oproj(y.reshape(m, D_INNER), l1_w_out, tm=OPROJ_TM, tn=OPROJ_TN, tk=OPROJ_TK,
                name="l1_oproj")
    return _ln_seq_rows(h1, br.reshape(bsz, T_TOT, D_MODEL), l1_ln_g, l1_ln_b, tr=LN_TR,
                        name="l1_ln")
```

```python
import functools

import numpy as np
import jax
import jax.numpy as jnp
from jax import lax
from jax.experimental import pallas as pl
from jax.experimental.pallas import tpu as pltpu

D_MODEL = 4096
SEQ = 2048
N_META = 16
T_TOT = SEQ + N_META
D_INNER = 2 * D_MODEL
HG_DK = 128
GLA_HEADS = 8
GLA_KW = D_INNER // 2
GLA_DK = GLA_KW // GLA_HEADS
GLA_DV = D_INNER // GLA_HEADS
GLA_RANK = 16
GLA_GATE_NORM = 16.0
DEPTH = 2
ALPHA = (2.0 * DEPTH) ** 0.25
LN_EPS = 1e-5
RMS_EPS = 1e-6
LOG2_E = 1.4426950408889634

SUB = 16
BLK = 128
LEVELS = (32, 64, 128)
N_SUB = BLK // SUB
LANE = 128
VMEM_LIMIT_V7X = 60 * 1024 * 1024

F32 = jnp.float32
BF16 = jnp.bfloat16


def _dot(a, b):
    return jnp.dot(a, b, preferred_element_type=F32)


def _dot_nt(a, b):
    return lax.dot_general(a, b, (((1,), (1,)), ((), ())), preferred_element_type=F32)


def _dot_tn(a, b):
    return lax.dot_general(a, b, (((0,), (0,)), ((), ())), preferred_element_type=F32)


def _sigmoid(x):
    return 1.0 / (1.0 + jnp.exp(-x))


def _split_bf16(x):
    hi = x.astype(BF16)
    return hi, (x - hi.astype(F32)).astype(BF16)


def _ep_identity(acc):
    return (acc,)


def _ep_silu(acc):
    return (acc * _sigmoid(acc),)


def _ep_scale(acc, *, scale):
    return (acc * scale,)


def _ep_silu_gain(acc, gain):
    return (acc * _sigmoid(acc) * gain,)


def _ep_hgrn2_gate(acc, b_f, lb_logits):
    e = jnp.exp(lb_logits - jnp.max(lb_logits, axis=0, keepdims=True))
    lb = e[0:1, :] / jnp.sum(e, axis=0, keepdims=True)
    fg = lb + (1.0 - lb) * _sigmoid(acc + b_f)
    hi, lo = _split_bf16(jnp.log2(fg))
    return hi, lo, 1.0 - fg


def _ep_gla_gate(acc, b_g):
    u = acc + b_g
    log_sig = jnp.minimum(u, 0.0) - jnp.log(1.0 + jnp.exp(-jnp.abs(u)))
    return _split_bf16(log_sig * (LOG2_E / GLA_GATE_NORM))


def _proj_kernel(a_ref, w_ref, *refs, epilogue, n_vec):
    vecs = [r[...] for r in refs[:n_vec]]
    acc = _dot(a_ref[...], w_ref[...].astype(BF16))
    for o_ref, val in zip(refs[n_vec:], epilogue(acc, *vecs)):
        o_ref[...] = val.astype(o_ref.dtype)


def _proj(a, w, col0, ncols, *, tm, tn, out_dtypes, epilogue=_ep_identity, vecs=(), name):
    m, k = a.shape
    assert m % tm == 0 and ncols % tn == 0 and col0 % tn == 0
    jb = col0 // tn
    outs = pl.pallas_call(
        functools.partial(_proj_kernel, epilogue=epilogue, n_vec=len(vecs)),
        out_shape=tuple(jax.ShapeDtypeStruct((m, ncols), dt) for dt in out_dtypes),
        grid=(m // tm, ncols // tn),
        in_specs=[
            pl.BlockSpec((tm, k), lambda i, j: (i, 0)),
            pl.BlockSpec((k, tn), lambda i, j: (0, j + jb)),
        ] + [pl.BlockSpec((vv.shape[0], tn), lambda i, j: (0, j)) for vv in vecs],
        out_specs=tuple(pl.BlockSpec((tm, tn), lambda i, j: (i, j)) for _ in out_dtypes),
        compiler_params=pltpu.CompilerParams(
            dimension_semantics=("arbitrary", "arbitrary"),
            vmem_limit_bytes=VMEM_LIMIT_V7X),
        name=name,
    )(a, w, *vecs)
    return outs if len(outs) > 1 else outs[0]


def _oproj_kernel(a_ref, w_ref, o_ref, acc_ref):
    kk = pl.program_id(2)

    @pl.when(kk == 0)
    def _():
        acc_ref[...] = jnp.zeros_like(acc_ref)

    acc_ref[...] += _dot(a_ref[...], w_ref[...].astype(BF16))

    @pl.when(kk == pl.num_programs(2) - 1)
    def _():
        o_ref[...] = acc_ref[...].astype(o_ref.dtype)


def _oproj(a, w, *, tm, tn, tk, name):
    m, k = a.shape
    n = w.shape[1]
    assert m % tm == 0 and n % tn == 0 and k % tk == 0
    return pl.pallas_call(
        _oproj_kernel,
        out_shape=jax.ShapeDtypeStruct((m, n), BF16),
        grid=(m // tm, n // tn, k // tk),
        in_specs=[
            pl.BlockSpec((tm, tk), lambda i, j, kk: (i, kk)),
            pl.BlockSpec((tk, tn), lambda i, j, kk: (kk, j)),
        ],
        out_specs=pl.BlockSpec((tm, tn), lambda i, j, kk: (i, j)),
        scratch_shapes=[pltpu.VMEM((tm, tn), F32)],
        compiler_params=pltpu.CompilerParams(
            dimension_semantics=("arbitrary", "arbitrary", "arbitrary"),
            vmem_limit_bytes=VMEM_LIMIT_V7X),
        name=name,
    )(a, w)


def _concat_cast_kernel(x_ref, meta_ref, o_ref):
    i = pl.program_id(1)

    @pl.when(i < pl.num_programs(1) - 1)
    def _():
        o_ref[...] = x_ref[...].astype(o_ref.dtype)

    @pl.when(i == pl.num_programs(1) - 1)
    def _():
        o_ref[0:N_META, :] = meta_ref[...].astype(o_ref.dtype)


def _concat_cast(x, meta, *, tr):
    bsz, _, d = x.shape
    n_x = SEQ // tr
    return pl.pallas_call(
        _concat_cast_kernel,
        out_shape=jax.ShapeDtypeStruct((bsz, T_TOT, d), BF16),
        grid=(bsz, n_x + 1),
        in_specs=[pl.BlockSpec((None, tr, d), lambda bi, i: (bi, jnp.minimum(i, n_x - 1), 0)),
                  pl.BlockSpec((N_META, d), lambda bi, i: (0, 0))],
        out_specs=pl.BlockSpec((None, tr, d), lambda bi, i: (bi, i, 0)),
        compiler_params=pltpu.CompilerParams(
            dimension_semantics=("arbitrary", "arbitrary"), vmem_limit_bytes=VMEM_LIMIT_V7X),
        name="concat_cast",
    )(x, meta)


def _ln_rows(h, br, g, b):
    r = ALPHA * h + br.astype(F32)
    mu = jnp.mean(r, axis=-1, keepdims=True)
    c = r - mu
    var = jnp.mean(c * c, axis=-1, keepdims=True)
    return c * lax.rsqrt(var + LN_EPS) * g + b


def _ln_first_kernel(x_ref, meta_ref, br_ref, g_ref, b_ref, o32_ref, o16_ref):
    i = pl.program_id(1)

    @pl.when(i < pl.num_programs(1) - 1)
    def _():
        y = _ln_rows(x_ref[...], br_ref[...], g_ref[...], b_ref[...])
        o32_ref[...] = y
        o16_ref[...] = y.astype(o16_ref.dtype)

    @pl.when(i == pl.num_programs(1) - 1)
    def _():
        y = _ln_rows(meta_ref[...], br_ref[0:N_META, :], g_ref[...], b_ref[...])
        o32_ref[0:N_META, :] = y
        o16_ref[0:N_META, :] = y.astype(o16_ref.dtype)


def _ln_first(x, meta, br, g, b, *, tr):
    bsz, _, d = x.shape
    n_x = SEQ // tr
    blk = pl.BlockSpec((None, tr, d), lambda bi, i: (bi, i, 0))
    vec = pl.BlockSpec((1, d), lambda bi, i: (0, 0))
    return pl.pallas_call(
        _ln_first_kernel,
        out_shape=(jax.ShapeDtypeStruct((bsz, T_TOT, d), F32),
                   jax.ShapeDtypeStruct((bsz, T_TOT, d), BF16)),
        grid=(bsz, n_x + 1),
        in_specs=[pl.BlockSpec((None, tr, d), lambda bi, i: (bi, jnp.minimum(i, n_x - 1), 0)),
                  pl.BlockSpec((N_META, d), lambda bi, i: (0, 0)), blk, vec, vec],
        out_specs=(blk, blk),
        compiler_params=pltpu.CompilerParams(
            dimension_semantics=("arbitrary", "arbitrary"), vmem_limit_bytes=VMEM_LIMIT_V7X),
        name="l0_ln",
    )(x, meta, br, g.reshape(1, d), b.reshape(1, d))


def _ln_kernel(h_ref, br_ref, g_ref, b_ref, o_ref):
    o_ref[...] = _ln_rows(h_ref[...], br_ref[...], g_ref[...], b_ref[...])


def _ln_seq_rows(h, br, g, b, *, tr, name):
    bsz, _, d = h.shape
    assert SEQ % tr == 0
    row = pl.BlockSpec((None, tr, d), lambda bi, i: (bi, i, 0))
    vec = pl.BlockSpec((1, d), lambda bi, i: (0, 0))
    return pl.pallas_call(
        _ln_kernel,
        out_shape=jax.ShapeDtypeStruct((bsz, SEQ, d), F32),
        grid=(bsz, SEQ // tr),
        in_specs=[row, row, vec, vec],
        out_specs=row,
        compiler_params=pltpu.CompilerParams(
            dimension_semantics=("arbitrary", "arbitrary"), vmem_limit_bytes=VMEM_LIMIT_V7X),
        name=name,
    )(h, br, g.reshape(1, d), b.reshape(1, d))


def _is_upper(p, g):
    return (p * SUB) % g >= g // 2


def _chunk_end_of_boundary(p, g):
    return ((p * SUB) // g * g + g // 2) // SUB - 1


def _row_scalar_table():
    pairs, index = [], {}

    def add(key, plus, minus):
        index[key] = len(pairs)
        pairs.append((plus, minus))

    last = N_SUB - 1
    for p in range(1, N_SUB):
        add(("q_dec", p), p - 1, None)
    for p in range(last):
        add(("k_dec", p), last, p)
    add(("g_last",), last, None)
    for g in LEVELS:
        for p in range(N_SUB):
            m = _chunk_end_of_boundary(p, g)
            if _is_upper(p, g) and m != p - 1:
                add(("up", g, p), p - 1, m)
            if not _is_upper(p, g) and m != p:
                add(("lo", g, p), m, p)
    return index, pairs


SCALAR_ROW, _SCALAR_PAIRS = _row_scalar_table()
N_SCALAR_ROWS = -(-len(_SCALAR_PAIRS) // SUB) * SUB


def _decay_consts():
    c = BLK
    i = np.arange(c)[:, None]
    s = np.arange(c)[None, :]
    summ = np.zeros((c + N_SCALAR_ROWS, c), np.float32)
    summ[:c] = s <= i
    for r, (plus, minus) in enumerate(_SCALAR_PAIRS):
        if plus is not None:
            summ[c + r, :SUB * (plus + 1)] += 1.0
        if minus is not None:
            summ[c + r, :SUB * (minus + 1)] -= 1.0
    lvl = np.full((c, c), -1, np.int32)
    lvl[((i // SUB) == (s // SUB)) & (s <= i)] = 0
    for l, g in enumerate(LEVELS):
        half = g // 2
        lvl[((i // g) == (s // g)) & ((i % g) >= half) & ((s % g) < half)] = l + 1
    return jnp.asarray(np.concatenate([summ, summ], axis=1), BF16), jnp.asarray(lvl)


def _pieces(x):
    return [x[p * SUB:(p + 1) * SUB, :] for p in range(N_SUB)]


def _rows(b, r, n):
    return jnp.broadcast_to(b[r:r + 1, :], (n, b.shape[1]))


def _scaled_operands(q, k, cum):
    n = q.shape[1]
    b = cum[:BLK]
    b_pc = _pieces(b)
    ends = [_rows(b, (p + 1) * SUB - 1, SUB) for p in range(N_SUB)]
    e0 = jnp.concatenate([b_pc[0]] + [b_pc[p] - ends[p - 1] for p in range(1, N_SUB)], axis=0)
    to_end = jnp.concatenate([ends[p] - b_pc[p] for p in range(N_SUB)], axis=0)
    scal = jnp.exp2(cum[BLK:])

    def factor(key):
        r = SCALAR_ROW[key]
        return jnp.broadcast_to(scal[r:r + 1, :], (SUB, n)).astype(BF16)

    def scaled(piece, key):
        return piece * factor(key) if key in SCALAR_ROW else piece

    q0 = q * jnp.exp2(e0).astype(BF16)
    k0 = k * jnp.exp2(-e0).astype(BF16)
    ky = k * jnp.exp2(to_end).astype(BF16)
    q0_pc, k_pc, ky_pc = _pieces(q0), _pieces(k), _pieces(ky)
    ops = {"q0": q0, "k0": k0, "q_up": [], "k_mix": []}
    for g in LEVELS:
        ops["q_up"].append(jnp.concatenate(
            [scaled(q0_pc[p], ("up", g, p)) for p in range(N_SUB) if _is_upper(p, g)], axis=0))
        ops["k_mix"].append(jnp.concatenate(
            [k_pc[p] if _is_upper(p, g) else scaled(ky_pc[p], ("lo", g, p))
             for p in range(N_SUB)], axis=0))
    ops["q_dec"] = jnp.concatenate([scaled(q0_pc[p], ("q_dec", p)) for p in range(N_SUB)], axis=0)
    ops["k_dec"] = jnp.concatenate([scaled(ky_pc[p], ("k_dec", p)) for p in range(N_SUB)], axis=0)
    r = SCALAR_ROW[("g_last",)]
    ops["g_last"] = scal[r:r + 1, :]
    return ops


def _masked_scores(raw, lvl, lvl_pc):
    sc_pc = _pieces(jnp.where(lvl == 0, raw[0], 0.0))
    for l, g in enumerate(LEVELS):
        ups = [p for p in range(N_SUB) if _is_upper(p, g)]
        for j, p in enumerate(ups):
            sc_pc[p] = jnp.where(lvl_pc[p] == l + 1, raw[l + 1][j * SUB:(j + 1) * SUB, :], sc_pc[p])
    return jnp.concatenate(sc_pc, axis=0).astype(BF16)


def _norm_gate(o, zg):
    ms = jnp.mean(o * o, axis=-1, keepdims=True)
    return o * lax.rsqrt(ms + RMS_EPS) * zg


def _attn_kernel(q_ref, k_ref, lh_ref, ll_ref, v_ref, zg_ref, sum_ref, lvl_ref, y_ref, st_ref,
                 *, heads, per):
    lvl = lvl_ref[...]
    lvl_pc = _pieces(lvl)
    summ = sum_ref[...]
    dk = q_ref.shape[1] // heads
    dv = v_ref.shape[1] // heads
    hs = range(heads)
    ksl = [slice(h * dk, (h + 1) * dk) for h in hs]
    vsl = [slice(h * dv, (h + 1) * dv) for h in hs]

    def pad_rows(t):
        return jnp.concatenate([t, jnp.zeros((BLK - N_META, t.shape[1]), t.dtype)], axis=0)

    def meta_block():
        rows = pl.ds(SEQ, N_META)
        b = _dot(summ[0:N_META, :], jnp.concatenate(
            [pad_rows(lh_ref[rows, :]), pad_rows(ll_ref[rows, :])], axis=0))
        q, k = q_ref[rows, :], k_ref[rows, :]
        q0 = q * jnp.exp2(b).astype(BF16)
        k0 = pad_rows(k * jnp.exp2(-b).astype(BF16))
        k_dec = pad_rows(k * jnp.exp2(b[N_META - 1:N_META, :] - b).astype(BF16))
        v = pad_rows(v_ref[rows, :])
        for h in hs:
            sc = jnp.where(lvl[0:N_META, :] == 0, _dot_nt(q0[:, ksl[h]], k0[:, ksl[h]]), 0.0)
            o = _dot(sc.astype(BF16), v[:, vsl[h]])
            st_ref[h] = _dot_tn(k_dec[:, ksl[h]], v[:, vsl[h]])
            y_ref[rows, vsl[h]] = _norm_gate(o, zg_ref[rows, vsl[h]].astype(F32)).astype(y_ref.dtype)

    def blocks(starts):
        nb = range(len(starts))
        rows = [pl.ds(r0, BLK) for r0 in starts]
        cum = [_dot(summ, jnp.concatenate([lh_ref[r, :], ll_ref[r, :]], axis=0)) for r in rows]
        ops = [_scaled_operands(q_ref[rows[n], :], k_ref[rows[n], :], cum[n]) for n in nb]
        vals = [v_ref[r, :] for r in rows]
        raw = [[[_dot_nt(ops[n]["q0"][:, ksl[h]], ops[n]["k0"][:, ksl[h]])]
                + [_dot_nt(ops[n]["q_up"][l][:, ksl[h]], ops[n]["k_mix"][l][:, ksl[h]])
                   for l in range(len(LEVELS))] for h in hs] for n in nb]
        sc = [[_masked_scores(raw[n][h], lvl, lvl_pc) for h in hs] for n in nb]
        upd = [[_dot_tn(ops[n]["k_dec"][:, ksl[h]], vals[n][:, vsl[h]]) for h in hs] for n in nb]
        g_col = [jnp.transpose(ops[n]["g_last"]) for n in nb]
        o = [[None] * heads for _ in nb]
        for n in nb:
            for h in hs:
                st = st_ref[h]
                o[n][h] = _dot(jnp.concatenate([sc[n][h], ops[n]["q_dec"][:, ksl[h]]], axis=1),
                               jnp.concatenate([vals[n][:, vsl[h]], st.astype(BF16)], axis=0))
                st_ref[h] = g_col[n][ksl[h], :] * st + upd[n][h]
        for n in nb:
            for h in hs:
                y = _norm_gate(o[n][h], zg_ref[rows[n], vsl[h]].astype(F32))
                y_ref[rows[n], vsl[h]] = y.astype(y_ref.dtype)

    meta_block()

    def step(i, carry):
        base = pl.multiple_of(i * (per * BLK), per * BLK)
        blocks([base + n * BLK for n in range(per)])
        return carry

    lax.fori_loop(0, SEQ // (per * BLK), step, 0)


def _attention(q, k, lg_hi, lg_lo, v, zg, *, heads, per, dk, dv, name):
    assert SEQ % (per * BLK) == 0
    bsz, t, _ = q.shape
    n_groups = v.shape[2] // (heads * dv)
    summ, lvl = _decay_consts()
    qk = pl.BlockSpec((None, t, heads * dk), lambda bi, g: (bi, 0, g))
    vz = pl.BlockSpec((None, t, heads * dv), lambda bi, g: (bi, 0, g))
    return pl.pallas_call(
        functools.partial(_attn_kernel, heads=heads, per=per),
        out_shape=jax.ShapeDtypeStruct(v.shape, BF16),
        grid=(bsz, n_groups),
        in_specs=[qk, qk, qk, qk, vz, vz,
                  pl.BlockSpec(summ.shape, lambda bi, g: (0, 0)),
                  pl.BlockSpec(lvl.shape, lambda bi, g: (0, 0))],
        out_specs=vz,
        scratch_shapes=[pltpu.VMEM((heads, dk, dv), F32)],
        compiler_params=pltpu.CompilerParams(
            dimension_semantics=("arbitrary", "arbitrary"), vmem_limit_bytes=VMEM_LIMIT_V7X),
        name=name,
    )(q, k, lg_hi, lg_lo, v, zg, summ, lvl)


PROJ_TM = 1376
PROJ_TN = 512
OPROJ_TM, OPROJ_TN, OPROJ_TK = 1376, 1024, 1024
LN_TR = 256
HG_HEADS_PER_STEP = 4


def kernel(x, meta, lb_logits, l0_w_in, l0_b_f, l0_norm_g, l0_w_out, l0_ln_g, l0_ln_b,
           l1_w_in, l1_w_g1, l1_w_g2, l1_b_g, l1_norm_g, l1_w_out, l1_ln_g, l1_ln_b):
    bsz = x.shape[0]
    m = bsz * T_TOT
    a0 = _concat_cast(x, meta, tr=LN_TR).reshape(m, D_MODEL)
    s3 = (bsz, T_TOT, D_INNER)
    row = lambda vv: vv.reshape(1, -1)

    proj0 = functools.partial(_proj, a0, l0_w_in, tm=PROJ_TM, tn=PROJ_TN)
    q = proj0(0 * D_INNER, D_INNER, out_dtypes=(BF16,), epilogue=_ep_silu, name="l0_proj_q")
    lg_hi, lg_lo, k = proj0(1 * D_INNER, D_INNER, out_dtypes=(BF16, BF16, BF16),
                            epilogue=_ep_hgrn2_gate, vecs=(row(l0_b_f), lb_logits),
                            name="l0_proj_f")
    v = proj0(2 * D_INNER, D_INNER, out_dtypes=(BF16,), name="l0_proj_i")
    zg = proj0(3 * D_INNER, D_INNER, out_dtypes=(BF16,), epilogue=_ep_silu_gain,
               vecs=(row(l0_norm_g),), name="l0_proj_z")
    y = _attention(*(t.reshape(s3) for t in (q, k, lg_hi, lg_lo, v, zg)),
                   heads=HG_HEADS_PER_STEP, per=4, dk=HG_DK, dv=HG_DK, name="hgrn2_attention")
    br = _oproj(y.reshape(m, D_INNER), l0_w_out, tm=OPROJ_TM, tn=OPROJ_TN, tk=OPROJ_TK,
                name="l0_oproj")
    h1, a1 = _ln_first(x, meta, br.reshape(bsz, T_TOT, D_MODEL), l0_ln_g, l0_ln_b, tr=LN_TR)
    a1 = a1.reshape(m, D_MODEL)

    proj1 = functools.partial(_proj, a1, l1_w_in, tm=PROJ_TM, tn=PROJ_TN)
    q = proj1(0, GLA_KW, out_dtypes=(BF16,),
              epilogue=functools.partial(_ep_scale, scale=GLA_DK ** -0.5), name="l1_proj_q")
    k = proj1(GLA_KW, GLA_KW, out_dtypes=(BF16,), name="l1_proj_k")
    v = proj1(2 * GLA_KW, D_INNER, out_dtypes=(BF16,), name="l1_proj_v")
    zg = proj1(2 * GLA_KW + D_INNER, D_INNER, out_dtypes=(BF16,), epilogue=_ep_silu_gain,
               vecs=(row(l1_norm_g),), name="l1_proj_z")
    w_g1p = jnp.pad(l1_w_g1, ((0, 0), (0, LANE - GLA_RANK)))
    w_g2p = jnp.pad(l1_w_g2, ((0, LANE - GLA_RANK), (0, 0)))
    g1 = _proj(a1, w_g1p, 0, LANE, tm=PROJ_TM, tn=LANE, out_dtypes=(BF16,), name="l1_proj_g1")
    lg_hi, lg_lo = _proj(g1, w_g2p, 0, GLA_KW, tm=PROJ_TM, tn=PROJ_TN, out_dtypes=(BF16, BF16),
                         epilogue=_ep_gla_gate, vecs=(row(l1_b_g),), name="l1_gate")
    s2 = (bsz, T_TOT, GLA_KW)
    y = _attention(q.reshape(s2), k.reshape(s2), lg_hi.reshape(s2), lg_lo.reshape(s2),
                   v.reshape(s3), zg.reshape(s3),
                   heads=1, per=2, dk=GLA_DK, dv=GLA_DV, name="gla_attention")
    br = _oproj(y.reshape(m, D_INNER), l1_w_out, tm=OPROJ_TM, tn=OPROJ_TN, tk=OPROJ_TK,
                name="l1_oproj")
    return _ln_seq_rows(h1, br.reshape(bsz, T_TOT, D_MODEL), l1_ln_g, l1_ln_b, tr=LN_TR,
                        name="l1_ln")
```

```python
import functools

import numpy as np
import jax
import jax.numpy as jnp
from jax import lax
from jax.experimental import pallas as pl
from jax.experimental.pallas import tpu as pltpu

D_MODEL = 4096
SEQ = 2048
N_META = 16
T_TOT = SEQ + N_META
D_INNER = 2 * D_MODEL
HG_DK = 128
GLA_HEADS = 8
GLA_KW = D_INNER // 2
GLA_DK = GLA_KW // GLA_HEADS
GLA_DV = D_INNER // GLA_HEADS
GLA_RANK = 16
GLA_GATE_NORM = 16.0
DEPTH = 2
ALPHA = (2.0 * DEPTH) ** 0.25
LN_EPS = 1e-5
RMS_EPS = 1e-6
LOG2_E = 1.4426950408889634

SUB = 16
BLK = 128
LEVELS = (32, 64, 128)
N_SUB = BLK // SUB
LANE = 128
VMEM_LIMIT_V7X = 60 * 1024 * 1024

F32 = jnp.float32
BF16 = jnp.bfloat16


def _dot(a, b):
    return jnp.dot(a, b, preferred_element_type=F32)


def _dot_nt(a, b):
    return lax.dot_general(a, b, (((1,), (1,)), ((), ())), preferred_element_type=F32)


def _dot_tn(a, b):
    return lax.dot_general(a, b, (((0,), (0,)), ((), ())), preferred_element_type=F32)


def _sigmoid(x):
    return 1.0 / (1.0 + jnp.exp(-x))


def _split_bf16(x):
    hi = x.astype(BF16)
    return hi, (x - hi.astype(F32)).astype(BF16)


def _ep_identity(acc):
    return (acc,)


def _ep_silu(acc):
    return (acc * _sigmoid(acc),)


def _ep_scale(acc, *, scale):
    return (acc * scale,)


def _ep_silu_gain(acc, gain):
    return (acc * _sigmoid(acc) * gain,)


def _ep_hgrn2_gate(acc, b_f, lb_logits):
    e = jnp.exp(lb_logits - jnp.max(lb_logits, axis=0, keepdims=True))
    lb = e[0:1, :] / jnp.sum(e, axis=0, keepdims=True)
    fg = lb + (1.0 - lb) * _sigmoid(acc + b_f)
    hi, lo = _split_bf16(jnp.log2(fg))
    return hi, lo, 1.0 - fg


def _ep_gla_gate(acc, b_g):
    u = acc + b_g
    log_sig = jnp.minimum(u, 0.0) - jnp.log(1.0 + jnp.exp(-jnp.abs(u)))
    return _split_bf16(log_sig * (LOG2_E / GLA_GATE_NORM))


def _proj_kernel(a_ref, w_ref, *refs, epilogue, n_vec):
    vecs = [r[...] for r in refs[:n_vec]]
    acc = _dot(a_ref[...], w_ref[...].astype(BF16))
    for o_ref, val in zip(refs[n_vec:], epilogue(acc, *vecs)):
        o_ref[...] = val.astype(o_ref.dtype)


def _proj(a, w, col0, ncols, *, tm, tn, out_dtypes, epilogue=_ep_identity, vecs=(), name):
    m, k = a.shape
    assert m % tm == 0 and ncols % tn == 0 and col0 % tn == 0
    jb = col0 // tn
    outs = pl.pallas_call(
        functools.partial(_proj_kernel, epilogue=epilogue, n_vec=len(vecs)),
        out_shape=tuple(jax.ShapeDtypeStruct((m, ncols), dt) for dt in out_dtypes),
        grid=(m // tm, ncols // tn),
        in_specs=[
            pl.BlockSpec((tm, k), lambda i, j: (i, 0)),
            pl.BlockSpec((k, tn), lambda i, j: (0, j + jb)),
        ] + [pl.BlockSpec((vv.shape[0], tn), lambda i, j: (0, j)) for vv in vecs],
        out_specs=tuple(pl.BlockSpec((tm, tn), lambda i, j: (i, j)) for _ in out_dtypes),
        compiler_params=pltpu.CompilerParams(
            dimension_semantics=("arbitrary", "arbitrary"),
            vmem_limit_bytes=VMEM_LIMIT_V7X),
        name=name,
    )(a, w, *vecs)
    return outs if len(outs) > 1 else outs[0]


def _oproj_kernel(a_ref, w_ref, o_ref, acc_ref):
    kk = pl.program_id(2)

    @pl.when(kk == 0)
    def _():
        acc_ref[...] = jnp.zeros_like(acc_ref)

    acc_ref[...] += _dot(a_ref[...], w_ref[...].astype(BF16))

    @pl.when(kk == pl.num_programs(2) - 1)
    def _():
        o_ref[...] = acc_ref[...].astype(o_ref.dtype)


def _oproj(a, w, *, tm, tn, tk, name):
    m, k = a.shape
    n = w.shape[1]
    assert m % tm == 0 and n % tn == 0 and k % tk == 0
    return pl.pallas_call(
        _oproj_kernel,
        out_shape=jax.ShapeDtypeStruct((m, n), BF16),
        grid=(m // tm, n // tn, k // tk),
        in_specs=[
            pl.BlockSpec((tm, tk), lambda i, j, kk: (i, kk)),
            pl.BlockSpec((tk, tn), lambda i, j, kk: (kk, j)),
        ],
        out_specs=pl.BlockSpec((tm, tn), lambda i, j, kk: (i, j)),
        scratch_shapes=[pltpu.VMEM((tm, tn), F32)],
        compiler_params=pltpu.CompilerParams(
            dimension_semantics=("arbitrary", "arbitrary", "arbitrary"),
            vmem_limit_bytes=VMEM_LIMIT_V7X),
        name=name,
    )(a, w)


def _concat_cast_kernel(x_ref, meta_ref, o_ref):
    i = pl.program_id(1)

    @pl.when(i < pl.num_programs(1) - 1)
    def _():
        o_ref[...] = x_ref[...].astype(o_ref.dtype)

    @pl.when(i == pl.num_programs(1) - 1)
    def _():
        o_ref[0:N_META, :] = meta_ref[...].astype(o_ref.dtype)


def _concat_cast(x, meta, *, tr):
    bsz, _, d = x.shape
    n_x = SEQ // tr
    return pl.pallas_call(
        _concat_cast_kernel,
        out_shape=jax.ShapeDtypeStruct((bsz, T_TOT, d), BF16),
        grid=(bsz, n_x + 1),
        in_specs=[pl.BlockSpec((None, tr, d), lambda bi, i: (bi, jnp.minimum(i, n_x - 1), 0)),
                  pl.BlockSpec((N_META, d), lambda bi, i: (0, 0))],
        out_specs=pl.BlockSpec((None, tr, d), lambda bi, i: (bi, i, 0)),
        compiler_params=pltpu.CompilerParams(
            dimension_semantics=("arbitrary", "arbitrary"), vmem_limit_bytes=VMEM_LIMIT_V7X),
        name="concat_cast",
    )(x, meta)


def _ln_rows(h, br, g, b):
    r = ALPHA * h + br.astype(F32)
    mu = jnp.mean(r, axis=-1, keepdims=True)
    c = r - mu
    var = jnp.mean(c * c, axis=-1, keepdims=True)
    return c * lax.rsqrt(var + LN_EPS) * g + b


def _ln_first_kernel(x_ref, meta_ref, br_ref, g_ref, b_ref, o32_ref, o16_ref):
    i = pl.program_id(1)

    @pl.when(i < pl.num_programs(1) - 1)
    def _():
        y = _ln_rows(x_ref[...], br_ref[...], g_ref[...], b_ref[...])
        o32_ref[...] = y
        o16_ref[...] = y.astype(o16_ref.dtype)

    @pl.when(i == pl.num_programs(1) - 1)
    def _():
        y = _ln_rows(meta_ref[...], br_ref[0:N_META, :], g_ref[...], b_ref[...])
        o32_ref[0:N_META, :] = y
        o16_ref[0:N_META, :] = y.astype(o16_ref.dtype)


def _ln_first(x, meta, br, g, b, *, tr):
    bsz, _, d = x.shape
    n_x = SEQ // tr
    blk = pl.BlockSpec((None, tr, d), lambda bi, i: (bi, i, 0))
    vec = pl.BlockSpec((1, d), lambda bi, i: (0, 0))
    return pl.pallas_call(
        _ln_first_kernel,
        out_shape=(jax.ShapeDtypeStruct((bsz, T_TOT, d), F32),
                   jax.ShapeDtypeStruct((bsz, T_TOT, d), BF16)),
        grid=(bsz, n_x + 1),
        in_specs=[pl.BlockSpec((None, tr, d), lambda bi, i: (bi, jnp.minimum(i, n_x - 1), 0)),
                  pl.BlockSpec((N_META, d), lambda bi, i: (0, 0)), blk, vec, vec],
        out_specs=(blk, blk),
        compiler_params=pltpu.CompilerParams(
            dimension_semantics=("arbitrary", "arbitrary"), vmem_limit_bytes=VMEM_LIMIT_V7X),
        name="l0_ln",
    )(x, meta, br, g.reshape(1, d), b.reshape(1, d))


def _ln_kernel(h_ref, br_ref, g_ref, b_ref, o_ref):
    o_ref[...] = _ln_rows(h_ref[...], br_ref[...], g_ref[...], b_ref[...])


def _ln_seq_rows(h, br, g, b, *, tr, name):
    bsz, _, d = h.shape
    assert SEQ % tr == 0
    row = pl.BlockSpec((None, tr, d), lambda bi, i: (bi, i, 0))
    vec = pl.BlockSpec((1, d), lambda bi, i: (0, 0))
    return pl.pallas_call(
        _ln_kernel,
        out_shape=jax.ShapeDtypeStruct((bsz, SEQ, d), F32),
        grid=(bsz, SEQ // tr),
        in_specs=[row, row, vec, vec],
        out_specs=row,
        compiler_params=pltpu.CompilerParams(
            dimension_semantics=("arbitrary", "arbitrary"), vmem_limit_bytes=VMEM_LIMIT_V7X),
        name=name,
    )(h, br, g.reshape(1, d), b.reshape(1, d))


def _is_upper(p, g):
    return (p * SUB) % g >= g // 2


def _chunk_end_of_boundary(p, g):
    return ((p * SUB) // g * g + g // 2) // SUB - 1


def _row_scalar_table():
    pairs, index = [], {}

    def add(key, plus, minus):
        index[key] = len(pairs)
        pairs.append((plus, minus))

    last = N_SUB - 1
    for p in range(1, N_SUB):
        add(("q_dec", p), p - 1, None)
    for p in range(last):
        add(("k_dec", p), last, p)
    add(("g_last",), last, None)
    for g in LEVELS:
        for p in range(N_SUB):
            m = _chunk_end_of_boundary(p, g)
            if _is_upper(p, g) and m != p - 1:
                add(("up", g, p), p - 1, m)
            if not _is_upper(p, g) and m != p:
                add(("lo", g, p), m, p)
    return index, pairs


SCALAR_ROW, _SCALAR_PAIRS = _row_scalar_table()
N_SCALAR_ROWS = -(-len(_SCALAR_PAIRS) // SUB) * SUB


def _decay_consts():
    c = BLK
    i = np.arange(c)[:, None]
    s = np.arange(c)[None, :]
    summ = np.zeros((c + N_SCALAR_ROWS, c), np.float32)
    summ[:c] = s <= i
    for r, (plus, minus) in enumerate(_SCALAR_PAIRS):
        if plus is not None:
            summ[c + r, :SUB * (plus + 1)] += 1.0
        if minus is not None:
            summ[c + r, :SUB * (minus + 1)] -= 1.0
    lvl = np.full((c, c), -1, np.int32)
    lvl[((i // SUB) == (s // SUB)) & (s <= i)] = 0
    for l, g in enumerate(LEVELS):
        half = g // 2
        lvl[((i // g) == (s // g)) & ((i % g) >= half) & ((s % g) < half)] = l + 1
    return jnp.asarray(np.concatenate([summ, summ], axis=1), BF16), jnp.asarray(lvl)


def _pieces(x):
    return [x[p * SUB:(p + 1) * SUB, :] for p in range(N_SUB)]


def _rows(b, r, n):
    return jnp.broadcast_to(b[r:r + 1, :], (n, b.shape[1]))


def _scaled_operands(q, k, cum):
    n = q.shape[1]
    b = cum[:BLK]
    b_pc = _pieces(b)
    ends = [_rows(b, (p + 1) * SUB - 1, SUB) for p in range(N_SUB)]
    e0 = jnp.concatenate([b_pc[0]] + [b_pc[p] - ends[p - 1] for p in range(1, N_SUB)], axis=0)
    to_end = jnp.concatenate([ends[p] - b_pc[p] for p in range(N_SUB)], axis=0)
    scal = jnp.exp2(cum[BLK:])

    def factor(key):
        r = SCALAR_ROW[key]
        return jnp.broadcast_to(scal[r:r + 1, :], (SUB, n)).astype(BF16)

    def scaled(piece, key):
        return piece * factor(key) if key in SCALAR_ROW else piece

    q0 = q * jnp.exp2(e0).astype(BF16)
    k0 = k * jnp.exp2(-e0).astype(BF16)
    ky = k * jnp.exp2(to_end).astype(BF16)
    q0_pc, k_pc, ky_pc = _pieces(q0), _pieces(k), _pieces(ky)
    ops = {"q0": q0, "k0": k0, "q_up": [], "k_mix": []}
    for g in LEVELS:
        ops["q_up"].append(jnp.concatenate(
            [scaled(q0_pc[p], ("up", g, p)) for p in range(N_SUB) if _is_upper(p, g)], axis=0))
        ops["k_mix"].append(jnp.concatenate(
            [k_pc[p] if _is_upper(p, g) else scaled(ky_pc[p], ("lo", g, p))
             for p in range(N_SUB)], axis=0))
    ops["q_dec"] = jnp.concatenate([scaled(q0_pc[p], ("q_dec", p)) for p in range(N_SUB)], axis=0)
    ops["k_dec"] = jnp.concatenate([scaled(ky_pc[p], ("k_dec", p)) for p in range(N_SUB)], axis=0)
    r = SCALAR_ROW[("g_last",)]
    ops["g_last"] = scal[r:r + 1, :]
    return ops


def _masked_scores(raw, lvl, lvl_pc):
    sc_pc = _pieces(jnp.where(lvl == 0, raw[0], 0.0))
    for l, g in enumerate(LEVELS):
        ups = [p for p in range(N_SUB) if _is_upper(p, g)]
        for j, p in enumerate(ups):
            sc_pc[p] = jnp.where(lvl_pc[p] == l + 1, raw[l + 1][j * SUB:(j + 1) * SUB, :], sc_pc[p])
    return jnp.concatenate(sc_pc, axis=0).astype(BF16)


def _norm_gate(o, zg):
    ms = jnp.mean(o * o, axis=-1, keepdims=True)
    return o * lax.rsqrt(ms + RMS_EPS) * zg


def _attn_kernel(q_ref, k_ref, lh_ref, ll_ref, v_ref, zg_ref, sum_ref, lvl_ref, y_ref, st_ref,
                 *, heads, per):
    lvl = lvl_ref[...]
    lvl_pc = _pieces(lvl)
    summ = sum_ref[...]
    dk = q_ref.shape[1] // heads
    dv = v_ref.shape[1] // heads
    hs = range(heads)
    ksl = [slice(h * dk, (h + 1) * dk) for h in hs]
    vsl = [slice(h * dv, (h + 1) * dv) for h in hs]

    def pad_rows(t):
        return jnp.concatenate([t, jnp.zeros((BLK - N_META, t.shape[1]), t.dtype)], axis=0)

    def meta_block():
        rows = pl.ds(SEQ, N_META)
        b = _dot(summ[0:N_META, :], jnp.concatenate(
            [pad_rows(lh_ref[rows, :]), pad_rows(ll_ref[rows, :])], axis=0))
        q, k = q_ref[rows, :], k_ref[rows, :]
        q0 = q * jnp.exp2(b).astype(BF16)
        k0 = pad_rows(k * jnp.exp2(-b).astype(BF16))
        k_dec = pad_rows(k * jnp.exp2(b[N_META - 1:N_META, :] - b).astype(BF16))
        v = pad_rows(v_ref[rows, :])
        for h in hs:
            sc = jnp.where(lvl[0:N_META, :] == 0, _dot_nt(q0[:, ksl[h]], k0[:, ksl[h]]), 0.0)
            o = _dot(sc.astype(BF16), v[:, vsl[h]])
            st_ref[h] = _dot_tn(k_dec[:, ksl[h]], v[:, vsl[h]])
            y_ref[rows, vsl[h]] = _norm_gate(o, zg_ref[rows, vsl[h]].astype(F32)).astype(y_ref.dtype)

    def blocks(starts):
        nb = len(starts)
        grp_a, grp_b = range(nb // 2), range(nb // 2, nb)
        rows = [pl.ds(r0, BLK) for r0 in starts]
        cum = [_dot(summ, jnp.concatenate([lh_ref[r, :], ll_ref[r, :]], axis=0)) for r in rows]
        ops = [[None] * heads for _ in range(nb)]
        raw = [[None] * heads for _ in range(nb)]
        sc = [[None] * heads for _ in range(nb)]
        upd = [[None] * heads for _ in range(nb)]
        o = [[None] * heads for _ in range(nb)]

        def prepare(n, h):
            ops[n][h] = _scaled_operands(q_ref[rows[n], ksl[h]], k_ref[rows[n], ksl[h]],
                                         cum[n][:, ksl[h]])

        def score(n, h):
            op = ops[n][h]
            raw[n][h] = [_dot_nt(op["q0"], op["k0"])] + [
                _dot_nt(op["q_up"][l], op["k_mix"][l]) for l in range(len(LEVELS))]

        def mask_and_update(n, h):
            sc[n][h] = _masked_scores(raw[n][h], lvl, lvl_pc)
            upd[n][h] = _dot_tn(ops[n][h]["k_dec"], v_ref[rows[n], vsl[h]])

        def output(n, h):
            st = st_ref[h]
            o[n][h] = _dot(jnp.concatenate([sc[n][h], ops[n][h]["q_dec"]], axis=1),
                           jnp.concatenate([v_ref[rows[n], vsl[h]], st.astype(BF16)], axis=0))
            st_ref[h] = jnp.transpose(ops[n][h]["g_last"]) * st + upd[n][h]

        def finish(n, h):
            y = _norm_gate(o[n][h], zg_ref[rows[n], vsl[h]].astype(F32))
            y_ref[rows[n], vsl[h]] = y.astype(y_ref.dtype)

        for n in grp_a:
            for h in hs:
                prepare(n, h)
        for n, m in zip(grp_a, grp_b):
            for h in hs:
                score(n, h)
                prepare(m, h)
        for n in grp_a:
            for h in hs:
                mask_and_update(n, h)
        for n, m in zip(grp_a, grp_b):
            for h in hs:
                output(n, h)
                score(m, h)
        for n, m in zip(grp_a, grp_b):
            for h in hs:
                finish(n, h)
                mask_and_update(m, h)
        for m in grp_b:
            for h in hs:
                output(m, h)
        for m in grp_b:
            for h in hs:
                finish(m, h)

    meta_block()

    def step(i, carry):
        base = pl.multiple_of(i * (per * BLK), per * BLK)
        blocks([base + n * BLK for n in range(per)])
        return carry

    lax.fori_loop(0, SEQ // (per * BLK), step, 0)


def _attention(q, k, lg_hi, lg_lo, v, zg, *, heads, per, dk, dv, name):
    assert SEQ % (per * BLK) == 0
    bsz, t, _ = q.shape
    n_groups = v.shape[2] // (heads * dv)
    summ, lvl = _decay_consts()
    qk = pl.BlockSpec((None, t, heads * dk), lambda bi, g: (bi, 0, g))
    vz = pl.BlockSpec((None, t, heads * dv), lambda bi, g: (bi, 0, g))
    return pl.pallas_call(
        functools.partial(_attn_kernel, heads=heads, per=per),
        out_shape=jax.ShapeDtypeStruct(v.shape, BF16),
        grid=(bsz, n_groups),
        in_specs=[qk, qk, qk, qk, vz, vz,
                  pl.BlockSpec(summ.shape, lambda bi, g: (0, 0)),
                  pl.BlockSpec(lvl.shape, lambda bi, g: (0, 0))],
        out_specs=vz,
        scratch_shapes=[pltpu.VMEM((heads, dk, dv), F32)],
        compiler_params=pltpu.CompilerParams(
            dimension_semantics=("arbitrary", "arbitrary"), vmem_limit_bytes=VMEM_LIMIT_V7X),
        name=name,
    )(q, k, lg_hi, lg_lo, v, zg, summ, lvl)


PROJ_TM = 1376
PROJ_TN = 512
OPROJ_TM, OPROJ_TN, OPROJ_TK = 1376, 1024, 1024
LN_TR = 256
HG_HEADS_PER_STEP = 4


def kernel(x, meta, lb_logits, l0_w_in, l0_b_f, l0_norm_g, l0_w_out, l0_ln_g, l0_ln_b,
           l1_w_in, l1_w_g1, l1_w_g2, l1_b_g, l1_norm_g, l1_w_out, l1_ln_g, l1_ln_b):
    bsz = x.shape[0]
    m = bsz * T_TOT
    a0 = _concat_cast(x, meta, tr=LN_TR).reshape(m, D_MODEL)
    s3 = (bsz, T_TOT, D_INNER)
    row = lambda vv: vv.reshape(1, -1)

    proj0 = functools.partial(_proj, a0, l0_w_in, tm=PROJ_TM, tn=PROJ_TN)
    q = proj0(0 * D_INNER, D_INNER, out_dtypes=(BF16,), epilogue=_ep_silu, name="l0_proj_q")
    lg_hi, lg_lo, k = proj0(1 * D_INNER, D_INNER, out_dtypes=(BF16, BF16, BF16),
                            epilogue=_ep_hgrn2_gate, vecs=(row(l0_b_f), lb_logits),
                            name="l0_proj_f")
    v = proj0(2 * D_INNER, D_INNER, out_dtypes=(BF16,), name="l0_proj_i")
    zg = proj0(3 * D_INNER, D_INNER, out_dtypes=(BF16,), epilogue=_ep_silu_gain,
               vecs=(row(l0_norm_g),), name="l0_proj_z")
    y = _attention(*(t.reshape(s3) for t in (q, k, lg_hi, lg_lo, v, zg)),
                   heads=HG_HEADS_PER_STEP, per=8, dk=HG_DK, dv=HG_DK, name="hgrn2_attention")
    br = _oproj(y.reshape(m, D_INNER), l0_w_out, tm=OPROJ_TM, tn=OPROJ_TN, tk=OPROJ_TK,
                name="l0_oproj")
    h1, a1 = _ln_first(x, meta, br.reshape(bsz, T_TOT, D_MODEL), l0_ln_g, l0_ln_b, tr=LN_TR)
    a1 = a1.reshape(m, D_MODEL)

    proj1 = functools.partial(_proj, a1, l1_w_in, tm=PROJ_TM, tn=PROJ_TN)
    q = proj1(0, GLA_KW, out_dtypes=(BF16,),
              epilogue=functools.partial(_ep_scale, scale=GLA_DK ** -0.5), name="l1_proj_q")
    k = proj1(GLA_KW, GLA_KW, out_dtypes=(BF16,), name="l1_proj_k")
    v = proj1(2 * GLA_KW, D_INNER, out_dtypes=(BF16,), name="l1_proj_v")
    zg = proj1(2 * GLA_KW + D_INNER, D_INNER, out_dtypes=(BF16,), epilogue=_ep_silu_gain,
               vecs=(row(l1_norm_g),), name="l1_proj_z")
    w_g1p = jnp.pad(l1_w_g1, ((0, 0), (0, LANE - GLA_RANK)))
    w_g2p = jnp.pad(l1_w_g2, ((0, LANE - GLA_RANK), (0, 0)))
    g1 = _proj(a1, w_g1p, 0, LANE, tm=PROJ_TM, tn=LANE, out_dtypes=(BF16,), name="l1_proj_g1")
    lg_hi, lg_lo = _proj(g1, w_g2p, 0, GLA_KW, tm=PROJ_TM, tn=PROJ_TN, out_dtypes=(BF16, BF16),
                         epilogue=_ep_gla_gate, vecs=(row(l1_b_g),), name="l1_gate")
    s2 = (bsz, T_TOT, GLA_KW)
    y = _attention(q.reshape(s2), k.reshape(s2), lg_hi.reshape(s2), lg_lo.reshape(s2),
                   v.reshape(s3), zg.reshape(s3),
                   heads=1, per=8, dk=GLA_DK, dv=GLA_DV, name="gla_attention")
    br = _oproj(y.reshape(m, D_INNER), l1_w_out, tm=OPROJ_TM, tn=OPROJ_TN, tk=OPROJ_TK,
                name="l1_oproj")
    return _ln_seq_rows(h1, br.reshape(bsz, T_TOT, D_MODEL), l1_ln_g, l1_ln_b, tr=LN_TR,
                        name="l1_ln")
```

```python
import functools

import numpy as np
import jax
import jax.numpy as jnp
from jax import lax
from jax.experimental import pallas as pl
from jax.experimental.pallas import tpu as pltpu

D_MODEL = 4096
SEQ = 2048
N_META = 16
T_TOT = SEQ + N_META
D_INNER = 2 * D_MODEL
HG_DK = 128
GLA_HEADS = 8
GLA_KW = D_INNER // 2
GLA_DK = GLA_KW // GLA_HEADS
GLA_DV = D_INNER // GLA_HEADS
GLA_RANK = 16
GLA_GATE_NORM = 16.0
DEPTH = 2
ALPHA = (2.0 * DEPTH) ** 0.25
LN_EPS = 1e-5
RMS_EPS = 1e-6
LOG2_E = 1.4426950408889634

SUB = 16
BLK = 128
LEVELS = (32, 64, 128)
N_SUB = BLK // SUB
LANE = 128
VMEM_LIMIT_V7X = 60 * 1024 * 1024

F32 = jnp.float32
BF16 = jnp.bfloat16


def _dot(a, b):
    return jnp.dot(a, b, preferred_element_type=F32)


def _dot_nt(a, b):
    return lax.dot_general(a, b, (((1,), (1,)), ((), ())), preferred_element_type=F32)


def _dot_tn(a, b):
    return lax.dot_general(a, b, (((0,), (0,)), ((), ())), preferred_element_type=F32)


def _sigmoid(x):
    return 0.5 + 0.5 * jnp.tanh(0.5 * x)


def _split_bf16(x):
    hi = x.astype(BF16)
    return hi, (x - hi.astype(F32)).astype(BF16)


def _ep_identity(acc):
    return (acc,)


def _ep_silu(acc):
    return (acc * _sigmoid(acc),)


def _ep_scale(acc, *, scale):
    return (acc * scale,)


def _ep_silu_gain(acc, gain):
    return (acc * _sigmoid(acc) * gain,)


def _ep_hgrn2_gate(acc, b_f, lb_logits):
    e = jnp.exp(lb_logits - jnp.max(lb_logits, axis=0, keepdims=True))
    lb = e[0:1, :] / jnp.sum(e, axis=0, keepdims=True)
    fg = lb + (1.0 - lb) * _sigmoid(acc + b_f)
    hi, lo = _split_bf16(jnp.log2(fg))
    return hi, lo, 1.0 - fg


def _ep_gla_gate(acc, b_g):
    u = acc + b_g
    log_sig = jnp.minimum(u, 0.0) - jnp.log(1.0 + jnp.exp(-jnp.abs(u)))
    return _split_bf16(log_sig * (LOG2_E / GLA_GATE_NORM))


def _proj_kernel(a_ref, w_ref, *refs, epilogue, n_vec):
    vecs = [r[...] for r in refs[:n_vec]]
    acc = _dot(a_ref[...], w_ref[...].astype(BF16))
    for o_ref, val in zip(refs[n_vec:], epilogue(acc, *vecs)):
        o_ref[...] = val.astype(o_ref.dtype)


def _proj(a, w, col0, ncols, *, tm, tn, out_dtypes, epilogue=_ep_identity, vecs=(), name):
    m, k = a.shape
    assert m % tm == 0 and ncols % tn == 0 and col0 % tn == 0
    jb = col0 // tn
    outs = pl.pallas_call(
        functools.partial(_proj_kernel, epilogue=epilogue, n_vec=len(vecs)),
        out_shape=tuple(jax.ShapeDtypeStruct((m, ncols), dt) for dt in out_dtypes),
        grid=(m // tm, ncols // tn),
        in_specs=[
            pl.BlockSpec((tm, k), lambda i, j: (i, 0)),
            pl.BlockSpec((k, tn), lambda i, j: (0, j + jb)),
        ] + [pl.BlockSpec((vv.shape[0], tn), lambda i, j: (0, j)) for vv in vecs],
        out_specs=tuple(pl.BlockSpec((tm, tn), lambda i, j: (i, j)) for _ in out_dtypes),
        compiler_params=pltpu.CompilerParams(
            dimension_semantics=("arbitrary", "arbitrary"),
            vmem_limit_bytes=VMEM_LIMIT_V7X),
        name=name,
    )(a, w, *vecs)
    return outs if len(outs) > 1 else outs[0]


def _oproj_kernel(a_ref, w_ref, o_ref, acc_ref):
    kk = pl.program_id(2)

    @pl.when(kk == 0)
    def _():
        acc_ref[...] = jnp.zeros_like(acc_ref)

    acc_ref[...] += _dot(a_ref[...], w_ref[...].astype(BF16))

    @pl.when(kk == pl.num_programs(2) - 1)
    def _():
        o_ref[...] = acc_ref[...].astype(o_ref.dtype)


def _oproj(a, w, *, tm, tn, tk, name):
    m, k = a.shape
    n = w.shape[1]
    assert m % tm == 0 and n % tn == 0 and k % tk == 0
    return pl.pallas_call(
        _oproj_kernel,
        out_shape=jax.ShapeDtypeStruct((m, n), BF16),
        grid=(m // tm, n // tn, k // tk),
        in_specs=[
            pl.BlockSpec((tm, tk), lambda i, j, kk: (i, kk)),
            pl.BlockSpec((tk, tn), lambda i, j, kk: (kk, j)),
        ],
        out_specs=pl.BlockSpec((tm, tn), lambda i, j, kk: (i, j)),
        scratch_shapes=[pltpu.VMEM((tm, tn), F32)],
        compiler_params=pltpu.CompilerParams(
            dimension_semantics=("arbitrary", "arbitrary", "arbitrary"),
            vmem_limit_bytes=VMEM_LIMIT_V7X),
        name=name,
    )(a, w)


def _concat_cast_kernel(x_ref, meta_ref, o_ref):
    i = pl.program_id(1)

    @pl.when(i < pl.num_programs(1) - 1)
    def _():
        o_ref[...] = x_ref[...].astype(o_ref.dtype)

    @pl.when(i == pl.num_programs(1) - 1)
    def _():
        o_ref[0:N_META, :] = meta_ref[...].astype(o_ref.dtype)


def _concat_cast(x, meta, *, tr):
    bsz, _, d = x.shape
    n_x = SEQ // tr
    return pl.pallas_call(
        _concat_cast_kernel,
        out_shape=jax.ShapeDtypeStruct((bsz, T_TOT, d), BF16),
        grid=(bsz, n_x + 1),
        in_specs=[pl.BlockSpec((None, tr, d), lambda bi, i: (bi, jnp.minimum(i, n_x - 1), 0)),
                  pl.BlockSpec((N_META, d), lambda bi, i: (0, 0))],
        out_specs=pl.BlockSpec((None, tr, d), lambda bi, i: (bi, i, 0)),
        compiler_params=pltpu.CompilerParams(
            dimension_semantics=("arbitrary", "arbitrary"), vmem_limit_bytes=VMEM_LIMIT_V7X),
        name="concat_cast",
    )(x, meta)


def _ln_rows(h, br, g, b):
    r = ALPHA * h + br.astype(F32)
    mu = jnp.mean(r, axis=-1, keepdims=True)
    c = r - mu
    var = jnp.mean(c * c, axis=-1, keepdims=True)
    return c * lax.rsqrt(var + LN_EPS) * g + b


def _ln_first_kernel(x_ref, meta_ref, br_ref, g_ref, b_ref, o32_ref, o16_ref):
    i = pl.program_id(1)

    @pl.when(i < pl.num_programs(1) - 1)
    def _():
        y = _ln_rows(x_ref[...], br_ref[...], g_ref[...], b_ref[...])
        o32_ref[...] = y
        o16_ref[...] = y.astype(o16_ref.dtype)

    @pl.when(i == pl.num_programs(1) - 1)
    def _():
        y = _ln_rows(meta_ref[...], br_ref[0:N_META, :], g_ref[...], b_ref[...])
        o32_ref[0:N_META, :] = y
        o16_ref[0:N_META, :] = y.astype(o16_ref.dtype)


def _ln_first(x, meta, br, g, b, *, tr):
    bsz, _, d = x.shape
    n_x = SEQ // tr
    blk = pl.BlockSpec((None, tr, d), lambda bi, i: (bi, i, 0))
    vec = pl.BlockSpec((1, d), lambda bi, i: (0, 0))
    return pl.pallas_call(
        _ln_first_kernel,
        out_shape=(jax.ShapeDtypeStruct((bsz, T_TOT, d), F32),
                   jax.ShapeDtypeStruct((bsz, T_TOT, d), BF16)),
        grid=(bsz, n_x + 1),
        in_specs=[pl.BlockSpec((None, tr, d), lambda bi, i: (bi, jnp.minimum(i, n_x - 1), 0)),
                  pl.BlockSpec((N_META, d), lambda bi, i: (0, 0)), blk, vec, vec],
        out_specs=(blk, blk),
        compiler_params=pltpu.CompilerParams(
            dimension_semantics=("arbitrary", "arbitrary"), vmem_limit_bytes=VMEM_LIMIT_V7X),
        name="l0_ln",
    )(x, meta, br, g.reshape(1, d), b.reshape(1, d))


def _ln_kernel(h_ref, br_ref, g_ref, b_ref, o_ref):
    o_ref[...] = _ln_rows(h_ref[...], br_ref[...], g_ref[...], b_ref[...])


def _ln_seq_rows(h, br, g, b, *, tr, name):
    bsz, _, d = h.shape
    assert SEQ % tr == 0
    row = pl.BlockSpec((None, tr, d), lambda bi, i: (bi, i, 0))
    vec = pl.BlockSpec((1, d), lambda bi, i: (0, 0))
    return pl.pallas_call(
        _ln_kernel,
        out_shape=jax.ShapeDtypeStruct((bsz, SEQ, d), F32),
        grid=(bsz, SEQ // tr),
        in_specs=[row, row, vec, vec],
        out_specs=row,
        compiler_params=pltpu.CompilerParams(
            dimension_semantics=("arbitrary", "arbitrary"), vmem_limit_bytes=VMEM_LIMIT_V7X),
        name=name,
    )(h, br, g.reshape(1, d), b.reshape(1, d))


def _is_upper(p, g):
    return (p * SUB) % g >= g // 2


def _chunk_end_of_boundary(p, g):
    return ((p * SUB) // g * g + g // 2) // SUB - 1


def _row_scalar_table():
    pairs, index = [], {}

    def add(key, plus, minus):
        index[key] = len(pairs)
        pairs.append((plus, minus))

    last = N_SUB - 1
    for p in range(1, N_SUB):
        add(("q_dec", p), p - 1, None)
    for p in range(last):
        add(("k_dec", p), last, p)
    add(("g_last",), last, None)
    for g in LEVELS:
        for p in range(N_SUB):
            m = _chunk_end_of_boundary(p, g)
            if _is_upper(p, g) and m != p - 1:
                add(("up", g, p), p - 1, m)
            if not _is_upper(p, g) and m != p:
                add(("lo", g, p), m, p)
    return index, pairs


SCALAR_ROW, _SCALAR_PAIRS = _row_scalar_table()
N_SCALAR_ROWS = -(-len(_SCALAR_PAIRS) // SUB) * SUB


def _decay_consts():
    c = BLK
    i = np.arange(c)[:, None]
    s = np.arange(c)[None, :]
    summ = np.zeros((c + N_SCALAR_ROWS, c), np.float32)
    summ[:c] = s <= i
    for r, (plus, minus) in enumerate(_SCALAR_PAIRS):
        if plus is not None:
            summ[c + r, :SUB * (plus + 1)] += 1.0
        if minus is not None:
            summ[c + r, :SUB * (minus + 1)] -= 1.0
    lvl = np.full((c, c), -1, np.int32)
    lvl[((i // SUB) == (s // SUB)) & (s <= i)] = 0
    for l, g in enumerate(LEVELS):
        half = g // 2
        lvl[((i // g) == (s // g)) & ((i % g) >= half) & ((s % g) < half)] = l + 1
    return jnp.asarray(np.concatenate([summ, summ], axis=1), BF16), jnp.asarray(lvl)


def _pieces(x):
    return [x[p * SUB:(p + 1) * SUB, :] for p in range(N_SUB)]


def _rows(b, r, n):
    return jnp.broadcast_to(b[r:r + 1, :], (n, b.shape[1]))


def _scaled_operands(q, k, cum):
    n = q.shape[1]
    b = cum[:BLK]
    b_pc = _pieces(b)
    ends = [_rows(b, (p + 1) * SUB - 1, SUB) for p in range(N_SUB)]
    e0 = jnp.concatenate([b_pc[0]] + [b_pc[p] - ends[p - 1] for p in range(1, N_SUB)], axis=0)
    to_end = jnp.concatenate([ends[p] - b_pc[p] for p in range(N_SUB)], axis=0)
    scal = jnp.exp2(cum[BLK:])

    def factor(key):
        r = SCALAR_ROW[key]
        return jnp.broadcast_to(scal[r:r + 1, :], (SUB, n)).astype(BF16)

    def scaled(piece, key):
        return piece * factor(key) if key in SCALAR_ROW else piece

    q0 = q * jnp.exp2(e0).astype(BF16)
    k0 = k * jnp.exp2(-e0).astype(BF16)
    ky = k * jnp.exp2(to_end).astype(BF16)
    q0_pc, k_pc, ky_pc = _pieces(q0), _pieces(k), _pieces(ky)
    ops = {"q0": q0, "k0": k0, "q_up": [], "k_mix": []}
    for g in LEVELS:
        ops["q_up"].append(jnp.concatenate(
            [scaled(q0_pc[p], ("up", g, p)) for p in range(N_SUB) if _is_upper(p, g)], axis=0))
        ops["k_mix"].append(jnp.concatenate(
            [k_pc[p] if _is_upper(p, g) else scaled(ky_pc[p], ("lo", g, p))
             for p in range(N_SUB)], axis=0))
    ops["q_dec"] = jnp.concatenate([scaled(q0_pc[p], ("q_dec", p)) for p in range(N_SUB)], axis=0)
    ops["k_dec"] = jnp.concatenate([scaled(ky_pc[p], ("k_dec", p)) for p in range(N_SUB)], axis=0)
    r = SCALAR_ROW[("g_last",)]
    ops["g_last"] = scal[r:r + 1, :]
    return ops


def _masked_scores(raw, lvl, lvl_pc):
    sc_pc = _pieces(jnp.where(lvl == 0, raw[0], 0.0))
    for l, g in enumerate(LEVELS):
        ups = [p for p in range(N_SUB) if _is_upper(p, g)]
        for j, p in enumerate(ups):
            sc_pc[p] = jnp.where(lvl_pc[p] == l + 1, raw[l + 1][j * SUB:(j + 1) * SUB, :], sc_pc[p])
    return jnp.concatenate(sc_pc, axis=0).astype(BF16)


def _norm_gate(o, zg):
    ms = jnp.mean(o * o, axis=-1, keepdims=True)
    return o * lax.rsqrt(ms + RMS_EPS) * zg


def _attn_kernel(q_ref, k_ref, lh_ref, ll_ref, v_ref, zg_ref, sum_ref, lvl_ref, y_ref, st_ref,
                 st_meta_ref, y_meta_ref, *, heads, per):
    lvl = lvl_ref[...]
    lvl_pc = _pieces(lvl)
    summ = sum_ref[...]
    dk = q_ref.shape[1] // heads
    dv = v_ref.shape[1] // heads
    hs = range(heads)
    ksl = [slice(h * dk, (h + 1) * dk) for h in hs]
    vsl = [slice(h * dv, (h + 1) * dv) for h in hs]

    def pad_rows(t):
        return jnp.concatenate([t, jnp.zeros((BLK - N_META, t.shape[1]), t.dtype)], axis=0)

    def meta_block():
        rows = pl.ds(SEQ, N_META)
        b = _dot(summ[0:N_META, :], jnp.concatenate(
            [pad_rows(lh_ref[rows, :]), pad_rows(ll_ref[rows, :])], axis=0))
        q, k = q_ref[rows, :], k_ref[rows, :]
        q0 = q * jnp.exp2(b).astype(BF16)
        k0 = pad_rows(k * jnp.exp2(-b).astype(BF16))
        k_dec = pad_rows(k * jnp.exp2(b[N_META - 1:N_META, :] - b).astype(BF16))
        v = pad_rows(v_ref[rows, :])
        for h in hs:
            sc = jnp.where(lvl[0:N_META, :] == 0, _dot_nt(q0[:, ksl[h]], k0[:, ksl[h]]), 0.0)
            o = _dot(sc.astype(BF16), v[:, vsl[h]])
            st_ref[h] = _dot_tn(k_dec[:, ksl[h]], v[:, vsl[h]])
            y_ref[rows, vsl[h]] = _norm_gate(o, zg_ref[rows, vsl[h]].astype(F32)).astype(y_ref.dtype)

    def blocks(starts):
        nb = len(starts)
        grp_a, grp_b = range(nb // 2), range(nb // 2, nb)
        rows = [pl.ds(r0, BLK) for r0 in starts]
        cum = [_dot(summ, jnp.concatenate([lh_ref[r, :], ll_ref[r, :]], axis=0)) for r in rows]
        ops = [[None] * heads for _ in range(nb)]
        raw = [[None] * heads for _ in range(nb)]
        sc = [[None] * heads for _ in range(nb)]
        upd = [[None] * heads for _ in range(nb)]
        o = [[None] * heads for _ in range(nb)]

        def prepare(n, h):
            ops[n][h] = _scaled_operands(q_ref[rows[n], ksl[h]], k_ref[rows[n], ksl[h]],
                                         cum[n][:, ksl[h]])

        def score(n, h):
            op = ops[n][h]
            raw[n][h] = [_dot_nt(op["q0"], op["k0"])] + [
                _dot_nt(op["q_up"][l], op["k_mix"][l]) for l in range(len(LEVELS))]

        def mask_and_update(n, h):
            sc[n][h] = _masked_scores(raw[n][h], lvl, lvl_pc)
            upd[n][h] = _dot_tn(ops[n][h]["k_dec"], v_ref[rows[n], vsl[h]])

        def output(n, h):
            st = st_ref[h]
            o[n][h] = _dot(jnp.concatenate([sc[n][h], ops[n][h]["q_dec"]], axis=1),
                           jnp.concatenate([v_ref[rows[n], vsl[h]], st.astype(BF16)], axis=0))
            st_ref[h] = jnp.transpose(ops[n][h]["g_last"]) * st + upd[n][h]

        def finish(n, h):
            y = _norm_gate(o[n][h], zg_ref[rows[n], vsl[h]].astype(F32))
            y_ref[rows[n], vsl[h]] = y.astype(y_ref.dtype)

        for n in grp_a:
            for h in hs:
                prepare(n, h)
        for n, m in zip(grp_a, grp_b):
            for h in hs:
                score(n, h)
                prepare(m, h)
        for n in grp_a:
            for h in hs:
                mask_and_update(n, h)
        for n, m in zip(grp_a, grp_b):
            for h in hs:
                output(n, h)
                score(m, h)
        for n, m in zip(grp_a, grp_b):
            for h in hs:
                finish(n, h)
                mask_and_update(m, h)
        for m in grp_b:
            for h in hs:
                output(m, h)
        for m in grp_b:
            for h in hs:
                finish(m, h)

    meta_rows = pl.ds(SEQ, N_META)

    @pl.when(pl.program_id(1) == 0)
    def _():
        meta_block()
        st_meta_ref[...] = st_ref[...]
        y_meta_ref[...] = y_ref[meta_rows, :]

    @pl.when(pl.program_id(1) > 0)
    def _():
        st_ref[...] = st_meta_ref[...]
        y_ref[meta_rows, :] = y_meta_ref[...]

    def step(i, carry):
        base = pl.multiple_of(i * (per * BLK), per * BLK)
        blocks([base + n * BLK for n in range(per)])
        return carry

    lax.fori_loop(0, SEQ // (per * BLK), step, 0)


def _attention(q, k, lg_hi, lg_lo, v, zg, *, heads, per, dk, dv, name):
    assert SEQ % (per * BLK) == 0
    bsz, t, _ = q.shape
    n_groups = v.shape[2] // (heads * dv)
    summ, lvl = _decay_consts()
    qk = pl.BlockSpec((None, t, heads * dk), lambda g, bi: (bi, 0, g))
    vz = pl.BlockSpec((None, t, heads * dv), lambda g, bi: (bi, 0, g))
    return pl.pallas_call(
        functools.partial(_attn_kernel, heads=heads, per=per),
        out_shape=jax.ShapeDtypeStruct(v.shape, BF16),
        grid=(n_groups, bsz),
        in_specs=[qk, qk, qk, qk, vz, vz,
                  pl.BlockSpec(summ.shape, lambda g, bi: (0, 0)),
                  pl.BlockSpec(lvl.shape, lambda g, bi: (0, 0))],
        out_specs=vz,
        scratch_shapes=[pltpu.VMEM((heads, dk, dv), F32), pltpu.VMEM((heads, dk, dv), F32),
                        pltpu.VMEM((N_META, heads * dv), BF16)],
        compiler_params=pltpu.CompilerParams(
            dimension_semantics=("arbitrary", "arbitrary"), vmem_limit_bytes=VMEM_LIMIT_V7X),
        name=name,
    )(q, k, lg_hi, lg_lo, v, zg, summ, lvl)


PROJ_TM = 1376
PROJ_TN = 512
OPROJ_TM, OPROJ_TN, OPROJ_TK = 2064, 1024, 1024
LN_TR = 256
HG_HEADS_PER_STEP = 4


def kernel(x, meta, lb_logits, l0_w_in, l0_b_f, l0_norm_g, l0_w_out, l0_ln_g, l0_ln_b,
           l1_w_in, l1_w_g1, l1_w_g2, l1_b_g, l1_norm_g, l1_w_out, l1_ln_g, l1_ln_b):
    bsz = x.shape[0]
    m = bsz * T_TOT
    a0 = _concat_cast(x, meta, tr=LN_TR).reshape(m, D_MODEL)
    s3 = (bsz, T_TOT, D_INNER)
    row = lambda vv: vv.reshape(1, -1)

    proj0 = functools.partial(_proj, a0, l0_w_in, tm=PROJ_TM, tn=PROJ_TN)
    q = proj0(0 * D_INNER, D_INNER, out_dtypes=(BF16,), epilogue=_ep_silu, name="l0_proj_q")
    lg_hi, lg_lo, k = proj0(1 * D_INNER, D_INNER, out_dtypes=(BF16, BF16, BF16),
                            epilogue=_ep_hgrn2_gate, vecs=(row(l0_b_f), lb_logits),
                            name="l0_proj_f")
    v = proj0(2 * D_INNER, D_INNER, out_dtypes=(BF16,), name="l0_proj_i")
    zg = proj0(3 * D_INNER, D_INNER, out_dtypes=(BF16,), epilogue=_ep_silu_gain,
               vecs=(row(l0_norm_g),), name="l0_proj_z")
    y = _attention(*(t.reshape(s3) for t in (q, k, lg_hi, lg_lo, v, zg)),
                   heads=HG_HEADS_PER_STEP, per=8, dk=HG_DK, dv=HG_DK, name="hgrn2_attention")
    br = _oproj(y.reshape(m, D_INNER), l0_w_out, tm=OPROJ_TM, tn=OPROJ_TN, tk=OPROJ_TK,
                name="l0_oproj")
    h1, a1 = _ln_first(x, meta, br.reshape(bsz, T_TOT, D_MODEL), l0_ln_g, l0_ln_b, tr=LN_TR)
    a1 = a1.reshape(m, D_MODEL)

    proj1 = functools.partial(_proj, a1, l1_w_in, tm=PROJ_TM, tn=PROJ_TN)
    q = proj1(0, GLA_KW, out_dtypes=(BF16,),
              epilogue=functools.partial(_ep_scale, scale=GLA_DK ** -0.5), name="l1_proj_q")
    k = proj1(GLA_KW, GLA_KW, out_dtypes=(BF16,), name="l1_proj_k")
    v = proj1(2 * GLA_KW, D_INNER, out_dtypes=(BF16,), name="l1_proj_v")
    zg = proj1(2 * GLA_KW + D_INNER, D_INNER, out_dtypes=(BF16,), epilogue=_ep_silu_gain,
               vecs=(row(l1_norm_g),), name="l1_proj_z")
    w_g1p = jnp.pad(l1_w_g1, ((0, 0), (0, LANE - GLA_RANK)))
    w_g2p = jnp.pad(l1_w_g2, ((0, LANE - GLA_RANK), (0, 0)))
    g1 = _proj(a1, w_g1p, 0, LANE, tm=PROJ_TM, tn=LANE, out_dtypes=(BF16,), name="l1_proj_g1")
    lg_hi, lg_lo = _proj(g1, w_g2p, 0, GLA_KW, tm=PROJ_TM, tn=PROJ_TN, out_dtypes=(BF16, BF16),
                         epilogue=_ep_gla_gate, vecs=(row(l1_b_g),), name="l1_gate")
    s2 = (bsz, T_TOT, GLA_KW)
    y = _attention(q.reshape(s2), k.reshape(s2), lg_hi.reshape(s2), lg_lo.reshape(s2),
                   v.reshape(s3), zg.reshape(s3),
                   heads=1, per=8, dk=GLA_DK, dv=GLA_DV, name="gla_attention")
    br = _oproj(y.reshape(m, D_INNER), l1_w_out, tm=OPROJ_TM, tn=OPROJ_TN, tk=OPROJ_TK,
                name="l1_oproj")
    return _ln_seq_rows(h1, br.reshape(bsz, T_TOT, D_MODEL), l1_ln_g, l1_ln_b, tr=LN_TR,
                        name="l1_ln")
```

```python
import functools

import numpy as np
import jax
import jax.numpy as jnp
from jax import lax
from jax.experimental import pallas as pl
from jax.experimental.pallas import tpu as pltpu

D_MODEL = 4096
SEQ = 2048
N_META = 16
T_TOT = SEQ + N_META
D_INNER = 2 * D_MODEL
HG_DK = 128
GLA_HEADS = 8
GLA_KW = D_INNER // 2
GLA_DK = GLA_KW // GLA_HEADS
GLA_DV = D_INNER // GLA_HEADS
GLA_RANK = 16
GLA_GATE_NORM = 16.0
DEPTH = 2
ALPHA = (2.0 * DEPTH) ** 0.25
LN_EPS = 1e-5
RMS_EPS = 1e-6
LOG2_E = 1.4426950408889634

SUB = 16
BLK = 128
LEVELS = (32, 64, 128)
N_SUB = BLK // SUB
LANE = 128
VMEM_LIMIT_V7X = 60 * 1024 * 1024

F32 = jnp.float32
BF16 = jnp.bfloat16


def _dot(a, b):
    return jnp.dot(a, b, preferred_element_type=F32)


def _dot_nt(a, b):
    return lax.dot_general(a, b, (((1,), (1,)), ((), ())), preferred_element_type=F32)


def _dot_tn(a, b):
    return lax.dot_general(a, b, (((0,), (0,)), ((), ())), preferred_element_type=F32)


def _sigmoid(x):
    return 0.5 + 0.5 * jnp.tanh(0.5 * x)


def _split_bf16(x):
    hi = x.astype(BF16)
    return hi, (x - hi.astype(F32)).astype(BF16)


def _ep_identity(acc):
    return (acc,)


def _ep_silu(acc):
    return (acc * _sigmoid(acc),)


def _ep_scale(acc, *, scale):
    return (acc * scale,)


def _ep_silu_gain(acc, gain):
    return (acc * _sigmoid(acc) * gain,)


def _ep_hgrn2_gate(acc, b_f, lb_logits):
    e = jnp.exp(lb_logits - jnp.max(lb_logits, axis=0, keepdims=True))
    lb = e[0:1, :] / jnp.sum(e, axis=0, keepdims=True)
    c = 0.5 * (1.0 - lb)
    ct = c * jnp.tanh(0.5 * (acc + b_f))
    hi, lo = _split_bf16(jnp.log2((lb + c) + ct))
    return hi, lo, c - ct


def _ep_gla_gate(acc, b_g):
    u = acc + b_g
    log_sig = jnp.minimum(u, 0.0) - jnp.log(1.0 + jnp.exp(-jnp.abs(u)))
    return _split_bf16(log_sig * (LOG2_E / GLA_GATE_NORM))


def _proj_kernel(a_ref, w_ref, *refs, epilogue, n_vec):
    vecs = [r[...] for r in refs[:n_vec]]
    acc = _dot(a_ref[...], w_ref[...].astype(BF16))
    for o_ref, val in zip(refs[n_vec:], epilogue(acc, *vecs)):
        o_ref[...] = val.astype(o_ref.dtype)


def _proj(a, w, col0, ncols, *, tm, tn, out_dtypes, epilogue=_ep_identity, vecs=(), name):
    m, k = a.shape
    assert m % tm == 0 and ncols % tn == 0 and col0 % tn == 0
    jb = col0 // tn
    outs = pl.pallas_call(
        functools.partial(_proj_kernel, epilogue=epilogue, n_vec=len(vecs)),
        out_shape=tuple(jax.ShapeDtypeStruct((m, ncols), dt) for dt in out_dtypes),
        grid=(m // tm, ncols // tn),
        in_specs=[
            pl.BlockSpec((tm, k), lambda i, j: (i, 0)),
            pl.BlockSpec((k, tn), lambda i, j: (0, j + jb)),
        ] + [pl.BlockSpec((vv.shape[0], tn), lambda i, j: (0, j)) for vv in vecs],
        out_specs=tuple(pl.BlockSpec((tm, tn), lambda i, j: (i, j)) for _ in out_dtypes),
        compiler_params=pltpu.CompilerParams(
            dimension_semantics=("arbitrary", "arbitrary"),
            vmem_limit_bytes=VMEM_LIMIT_V7X),
        name=name,
    )(a, w, *vecs)
    return outs if len(outs) > 1 else outs[0]


def _oproj_kernel(a_ref, w_ref, o_ref, acc_ref):
    kk = pl.program_id(2)

    @pl.when(kk == 0)
    def _():
        acc_ref[...] = jnp.zeros_like(acc_ref)

    acc_ref[...] += _dot(a_ref[...], w_ref[...].astype(BF16))

    @pl.when(kk == pl.num_programs(2) - 1)
    def _():
        o_ref[...] = acc_ref[...].astype(o_ref.dtype)


def _oproj(a, w, *, tm, tn, tk, name):
    m, k = a.shape
    n = w.shape[1]
    assert m % tm == 0 and n % tn == 0 and k % tk == 0
    return pl.pallas_call(
        _oproj_kernel,
        out_shape=jax.ShapeDtypeStruct((m, n), BF16),
        grid=(m // tm, n // tn, k // tk),
        in_specs=[
            pl.BlockSpec((tm, tk), lambda i, j, kk: (i, kk)),
            pl.BlockSpec((tk, tn), lambda i, j, kk: (kk, j)),
        ],
        out_specs=pl.BlockSpec((tm, tn), lambda i, j, kk: (i, j)),
        scratch_shapes=[pltpu.VMEM((tm, tn), F32)],
        compiler_params=pltpu.CompilerParams(
            dimension_semantics=("arbitrary", "arbitrary", "arbitrary"),
            vmem_limit_bytes=VMEM_LIMIT_V7X),
        name=name,
    )(a, w)


def _concat_cast_kernel(x_ref, meta_ref, o_ref):
    i = pl.program_id(1)

    @pl.when(i < pl.num_programs(1) - 1)
    def _():
        o_ref[...] = x_ref[...].astype(o_ref.dtype)

    @pl.when(i == pl.num_programs(1) - 1)
    def _():
        o_ref[0:N_META, :] = meta_ref[...].astype(o_ref.dtype)


def _concat_cast(x, meta, *, tr):
    bsz, _, d = x.shape
    n_x = SEQ // tr
    return pl.pallas_call(
        _concat_cast_kernel,
        out_shape=jax.ShapeDtypeStruct((bsz, T_TOT, d), BF16),
        grid=(bsz, n_x + 1),
        in_specs=[pl.BlockSpec((None, tr, d), lambda bi, i: (bi, jnp.minimum(i, n_x - 1), 0)),
                  pl.BlockSpec((N_META, d), lambda bi, i: (0, 0))],
        out_specs=pl.BlockSpec((None, tr, d), lambda bi, i: (bi, i, 0)),
        compiler_params=pltpu.CompilerParams(
            dimension_semantics=("arbitrary", "arbitrary"), vmem_limit_bytes=VMEM_LIMIT_V7X),
        name="concat_cast",
    )(x, meta)


def _ln_rows(h, br, g, b):
    r = ALPHA * h + br.astype(F32)
    mu = jnp.mean(r, axis=-1, keepdims=True)
    c = r - mu
    var = jnp.mean(c * c, axis=-1, keepdims=True)
    return c * lax.rsqrt(var + LN_EPS) * g + b


def _ln_first_kernel(x_ref, meta_ref, br_ref, g_ref, b_ref, o32_ref, o16_ref):
    i = pl.program_id(1)

    @pl.when(i < pl.num_programs(1) - 1)
    def _():
        y = _ln_rows(x_ref[...], br_ref[...], g_ref[...], b_ref[...])
        o32_ref[...] = y
        o16_ref[...] = y.astype(o16_ref.dtype)

    @pl.when(i == pl.num_programs(1) - 1)
    def _():
        y = _ln_rows(meta_ref[...], br_ref[0:N_META, :], g_ref[...], b_ref[...])
        o32_ref[0:N_META, :] = y
        o16_ref[0:N_META, :] = y.astype(o16_ref.dtype)


def _ln_first(x, meta, br, g, b, *, tr):
    bsz, _, d = x.shape
    n_x = SEQ // tr
    blk = pl.BlockSpec((None, tr, d), lambda bi, i: (bi, i, 0))
    vec = pl.BlockSpec((1, d), lambda bi, i: (0, 0))
    return pl.pallas_call(
        _ln_first_kernel,
        out_shape=(jax.ShapeDtypeStruct((bsz, T_TOT, d), F32),
                   jax.ShapeDtypeStruct((bsz, T_TOT, d), BF16)),
        grid=(bsz, n_x + 1),
        in_specs=[pl.BlockSpec((None, tr, d), lambda bi, i: (bi, jnp.minimum(i, n_x - 1), 0)),
                  pl.BlockSpec((N_META, d), lambda bi, i: (0, 0)), blk, vec, vec],
        out_specs=(blk, blk),
        compiler_params=pltpu.CompilerParams(
            dimension_semantics=("arbitrary", "arbitrary"), vmem_limit_bytes=VMEM_LIMIT_V7X),
        name="l0_ln",
    )(x, meta, br, g.reshape(1, d), b.reshape(1, d))


def _ln_kernel(h_ref, br_ref, g_ref, b_ref, o_ref):
    o_ref[...] = _ln_rows(h_ref[...], br_ref[...], g_ref[...], b_ref[...])


def _ln_seq_rows(h, br, g, b, *, tr, name):
    bsz, _, d = h.shape
    assert SEQ % tr == 0
    row = pl.BlockSpec((None, tr, d), lambda bi, i: (bi, i, 0))
    vec = pl.BlockSpec((1, d), lambda bi, i: (0, 0))
    return pl.pallas_call(
        _ln_kernel,
        out_shape=jax.ShapeDtypeStruct((bsz, SEQ, d), F32),
        grid=(bsz, SEQ // tr),
        in_specs=[row, row, vec, vec],
        out_specs=row,
        compiler_params=pltpu.CompilerParams(
            dimension_semantics=("arbitrary", "arbitrary"), vmem_limit_bytes=VMEM_LIMIT_V7X),
        name=name,
    )(h, br, g.reshape(1, d), b.reshape(1, d))


def _is_upper(p, g):
    return (p * SUB) % g >= g // 2


def _chunk_end_of_boundary(p, g):
    return ((p * SUB) // g * g + g // 2) // SUB - 1


def _row_scalar_table():
    pairs, index = [], {}

    def add(key, plus, minus):
        index[key] = len(pairs)
        pairs.append((plus, minus))

    last = N_SUB - 1
    for p in range(1, N_SUB):
        add(("q_dec", p), p - 1, None)
    for p in range(last):
        add(("k_dec", p), last, p)
    add(("g_last",), last, None)
    for g in LEVELS:
        for p in range(N_SUB):
            m = _chunk_end_of_boundary(p, g)
            if _is_upper(p, g) and m != p - 1:
                add(("up", g, p), p - 1, m)
            if not _is_upper(p, g) and m != p:
                add(("lo", g, p), m, p)
    return index, pairs


SCALAR_ROW, _SCALAR_PAIRS = _row_scalar_table()
N_SCALAR_ROWS = -(-len(_SCALAR_PAIRS) // SUB) * SUB


def _decay_consts():
    c = BLK
    i = np.arange(c)[:, None]
    s = np.arange(c)[None, :]
    summ = np.zeros((c + N_SCALAR_ROWS, c), np.float32)
    summ[:c] = s <= i
    for r, (plus, minus) in enumerate(_SCALAR_PAIRS):
        if plus is not None:
            summ[c + r, :SUB * (plus + 1)] += 1.0
        if minus is not None:
            summ[c + r, :SUB * (minus + 1)] -= 1.0
    lvl = np.full((c, c), -1, np.int32)
    lvl[((i // SUB) == (s // SUB)) & (s <= i)] = 0
    for l, g in enumerate(LEVELS):
        half = g // 2
        lvl[((i // g) == (s // g)) & ((i % g) >= half) & ((s % g) < half)] = l + 1
    return jnp.asarray(np.concatenate([summ, summ], axis=1), BF16), jnp.asarray(lvl)


def _pieces(x):
    return [x[p * SUB:(p + 1) * SUB, :] for p in range(N_SUB)]


def _rows(b, r, n):
    return jnp.broadcast_to(b[r:r + 1, :], (n, b.shape[1]))


def _scaled_operands(q, k, cum):
    n = q.shape[1]
    b = cum[:BLK]
    b_pc = _pieces(b)
    ends = [_rows(b, (p + 1) * SUB - 1, SUB) for p in range(N_SUB)]
    e0 = jnp.concatenate([b_pc[0]] + [b_pc[p] - ends[p - 1] for p in range(1, N_SUB)], axis=0)
    to_end = jnp.concatenate([ends[p] - b_pc[p] for p in range(N_SUB)], axis=0)
    scal = jnp.exp2(cum[BLK:])

    def factor(key):
        r = SCALAR_ROW[key]
        return jnp.broadcast_to(scal[r:r + 1, :], (SUB, n)).astype(BF16)

    def scaled(piece, key):
        return piece * factor(key) if key in SCALAR_ROW else piece

    q0 = q * jnp.exp2(e0).astype(BF16)
    k0 = k * jnp.exp2(-e0).astype(BF16)
    ky = k * jnp.exp2(to_end).astype(BF16)
    q0_pc, k_pc, ky_pc = _pieces(q0), _pieces(k), _pieces(ky)
    ops = {"q0": q0, "k0": k0, "q_up": [], "k_mix": []}
    for g in LEVELS:
        ops["q_up"].append(jnp.concatenate(
            [scaled(q0_pc[p], ("up", g, p)) for p in range(N_SUB) if _is_upper(p, g)], axis=0))
        ops["k_mix"].append(jnp.concatenate(
            [k_pc[p] if _is_upper(p, g) else scaled(ky_pc[p], ("lo", g, p))
             for p in range(N_SUB)], axis=0))
    ops["q_dec"] = jnp.concatenate([scaled(q0_pc[p], ("q_dec", p)) for p in range(N_SUB)], axis=0)
    ops["k_dec"] = jnp.concatenate([scaled(ky_pc[p], ("k_dec", p)) for p in range(N_SUB)], axis=0)
    r = SCALAR_ROW[("g_last",)]
    ops["g_last"] = scal[r:r + 1, :]
    return ops


def _masked_scores(raw, lvl, lvl_pc):
    sc_pc = _pieces(jnp.where(lvl == 0, raw[0], 0.0))
    for l, g in enumerate(LEVELS):
        ups = [p for p in range(N_SUB) if _is_upper(p, g)]
        for j, p in enumerate(ups):
            sc_pc[p] = jnp.where(lvl_pc[p] == l + 1, raw[l + 1][j * SUB:(j + 1) * SUB, :], sc_pc[p])
    return jnp.concatenate(sc_pc, axis=0).astype(BF16)


def _norm_gate(o, zg):
    ms = jnp.mean(o * o, axis=-1, keepdims=True)
    return o * lax.rsqrt(ms + RMS_EPS) * zg


def _attn_kernel(q_ref, k_ref, lh_ref, ll_ref, v_ref, zg_ref, sum_ref, lvl_ref, y_ref, st_ref,
                 st_meta_ref, y_meta_ref, *, heads, pack, per):
    lvl = lvl_ref[...]
    lvl_pc = _pieces(lvl)
    summ = sum_ref[...]
    dk = q_ref.shape[1] // heads
    dv = v_ref.shape[1] // heads
    units = range(heads // pack)
    members = range(pack)
    ksl = [slice(u * pack * dk, (u + 1) * pack * dk) for u in units]
    vsl = [slice(u * pack * dv, (u + 1) * pack * dv) for u in units]
    kin = [slice(p * dk, (p + 1) * dk) for p in members]
    vin = [slice(p * dv, (p + 1) * dv) for p in members]

    def block_diag(x):
        if pack == 1:
            return x
        zero = jnp.zeros((x.shape[0], dk), x.dtype)
        return jnp.concatenate(
            [jnp.concatenate([x[:, kin[p]] if c == p else zero for c in members], axis=1)
             for p in members], axis=0)

    def pad_rows(t):
        return jnp.concatenate([t, jnp.zeros((BLK - N_META, t.shape[1]), t.dtype)], axis=0)

    def meta_block():
        rows = pl.ds(SEQ, N_META)
        b = _dot(summ[0:N_META, :], jnp.concatenate(
            [pad_rows(lh_ref[rows, :]), pad_rows(ll_ref[rows, :])], axis=0))
        q, k = q_ref[rows, :], k_ref[rows, :]
        q0 = q * jnp.exp2(b).astype(BF16)
        k0 = pad_rows(k * jnp.exp2(-b).astype(BF16))
        k_dec = pad_rows(k * jnp.exp2(b[N_META - 1:N_META, :] - b).astype(BF16))
        v = pad_rows(v_ref[rows, :])
        for u in units:
            for p in members:
                kh = slice(ksl[u].start + p * dk, ksl[u].start + (p + 1) * dk)
                vh = slice(vsl[u].start + p * dv, vsl[u].start + (p + 1) * dv)
                sc = jnp.where(lvl[0:N_META, :] == 0, _dot_nt(q0[:, kh], k0[:, kh]), 0.0)
                o = _dot(sc.astype(BF16), v[:, vh])
                st_ref[u, :, vin[p]] = _dot_tn(k_dec[:, kh], v[:, vh])
                y_ref[rows, vh] = _norm_gate(o, zg_ref[rows, vh].astype(F32)).astype(y_ref.dtype)

    def blocks(starts):
        nb = len(starts)
        grp_a, grp_b = range(nb // 2), range(nb // 2, nb)
        rows = [pl.ds(r0, BLK) for r0 in starts]
        cum = [_dot(summ, jnp.concatenate([lh_ref[r, :], ll_ref[r, :]], axis=0)) for r in rows]
        ops = [[None] * len(units) for _ in range(nb)]
        raw = [[None] * len(units) for _ in range(nb)]
        sc = [[None] * len(units) for _ in range(nb)]
        upd = [[None] * len(units) for _ in range(nb)]
        o = [[None] * len(units) for _ in range(nb)]

        def prepare(n, u):
            ops[n][u] = _scaled_operands(q_ref[rows[n], ksl[u]], k_ref[rows[n], ksl[u]],
                                         cum[n][:, ksl[u]])

        def score(n, u):
            op = ops[n][u]
            raw[n][u] = [_dot_nt(block_diag(op["q0"]), op["k0"])] + [
                _dot_nt(block_diag(op["q_up"][l]), op["k_mix"][l]) for l in range(len(LEVELS))]

        def mask_and_update(n, u):
            sc[n][u] = [_masked_scores(
                [r[p * (r.shape[0] // pack):(p + 1) * (r.shape[0] // pack), :] for r in raw[n][u]],
                lvl, lvl_pc) for p in members]
            upd[n][u] = _dot_tn(ops[n][u]["k_dec"], v_ref[rows[n], vsl[u]])

        def output(n, u):
            st = st_ref[u]
            lhs = jnp.concatenate(
                [jnp.concatenate([sc[n][u][p], ops[n][u]["q_dec"][:, kin[p]]], axis=1)
                 for p in members], axis=0)
            res = _dot(lhs, jnp.concatenate([v_ref[rows[n], vsl[u]], st.astype(BF16)], axis=0))
            o[n][u] = [res[p * BLK:(p + 1) * BLK, vin[p]] for p in members]
            for p in members:
                g_col = jnp.transpose(ops[n][u]["g_last"][:, kin[p]])
                st_ref[u, :, vin[p]] = g_col * st[:, vin[p]] + upd[n][u][kin[p], vin[p]]

        def finish(n, u):
            for p in members:
                vh = slice(vsl[u].start + p * dv, vsl[u].start + (p + 1) * dv)
                y = _norm_gate(o[n][u][p], zg_ref[rows[n], vh].astype(F32))
                y_ref[rows[n], vh] = y.astype(y_ref.dtype)

        for n in grp_a:
            for u in units:
                prepare(n, u)
        for n, m in zip(grp_a, grp_b):
            for u in units:
                score(n, u)
                prepare(m, u)
        for n in grp_a:
            for u in units:
                mask_and_update(n, u)
        for n, m in zip(grp_a, grp_b):
            for u in units:
                output(n, u)
                score(m, u)
        for n, m in zip(grp_a, grp_b):
            for u in units:
                finish(n, u)
                mask_and_update(m, u)
        for m in grp_b:
            for u in units:
                output(m, u)
        for m in grp_b:
            for u in units:
                finish(m, u)

    meta_rows = pl.ds(SEQ, N_META)

    @pl.when(pl.program_id(1) == 0)
    def _():
        meta_block()
        st_meta_ref[...] = st_ref[...]
        y_meta_ref[...] = y_ref[meta_rows, :]

    @pl.when(pl.program_id(1) > 0)
    def _():
        st_ref[...] = st_meta_ref[...]
        y_ref[meta_rows, :] = y_meta_ref[...]

    def step(i, carry):
        base = pl.multiple_of(i * (per * BLK), per * BLK)
        blocks([base + n * BLK for n in range(per)])
        return carry

    lax.fori_loop(0, SEQ // (per * BLK), step, 0)


def _attention(q, k, lg_hi, lg_lo, v, zg, *, heads, pack, per, dk, dv, name):
    assert heads % pack == 0
    assert SEQ % (per * BLK) == 0
    bsz, t, _ = q.shape
    n_groups = v.shape[2] // (heads * dv)
    summ, lvl = _decay_consts()
    qk = pl.BlockSpec((None, t, heads * dk), lambda g, bi: (bi, 0, g))
    vz = pl.BlockSpec((None, t, heads * dv), lambda g, bi: (bi, 0, g))
    return pl.pallas_call(
        functools.partial(_attn_kernel, heads=heads, pack=pack, per=per),
        out_shape=jax.ShapeDtypeStruct(v.shape, BF16),
        grid=(n_groups, bsz),
        in_specs=[qk, qk, qk, qk, vz, vz,
                  pl.BlockSpec(summ.shape, lambda g, bi: (0, 0)),
                  pl.BlockSpec(lvl.shape, lambda g, bi: (0, 0))],
        out_specs=vz,
        scratch_shapes=[pltpu.VMEM((heads // pack, dk, pack * dv), F32),
                        pltpu.VMEM((heads // pack, dk, pack * dv), F32),
                        pltpu.VMEM((N_META, heads * dv), BF16)],
        compiler_params=pltpu.CompilerParams(
            dimension_semantics=("arbitrary", "arbitrary"), vmem_limit_bytes=VMEM_LIMIT_V7X),
        name=name,
    )(q, k, lg_hi, lg_lo, v, zg, summ, lvl)


PROJ_TM = 1376
PROJ_TN = 512
OPROJ_TM, OPROJ_TN, OPROJ_TK = 2064, 1024, 1024
LN_TR = 512
GATE_TN = 1024
HG_HEADS_PER_STEP = 4


def kernel(x, meta, lb_logits, l0_w_in, l0_b_f, l0_norm_g, l0_w_out, l0_ln_g, l0_ln_b,
           l1_w_in, l1_w_g1, l1_w_g2, l1_b_g, l1_norm_g, l1_w_out, l1_ln_g, l1_ln_b):
    bsz = x.shape[0]
    m = bsz * T_TOT
    a0 = _concat_cast(x, meta, tr=LN_TR).reshape(m, D_MODEL)
    s3 = (bsz, T_TOT, D_INNER)
    row = lambda vv: vv.reshape(1, -1)

    proj0 = functools.partial(_proj, a0, l0_w_in, tm=PROJ_TM, tn=PROJ_TN)
    q = proj0(0 * D_INNER, D_INNER, out_dtypes=(BF16,), epilogue=_ep_silu, name="l0_proj_q")
    lg_hi, lg_lo, k = proj0(1 * D_INNER, D_INNER, out_dtypes=(BF16, BF16, BF16),
                            epilogue=_ep_hgrn2_gate, vecs=(row(l0_b_f), lb_logits),
                            name="l0_proj_f")
    v = proj0(2 * D_INNER, D_INNER, out_dtypes=(BF16,), name="l0_proj_i")
    zg = proj0(3 * D_INNER, D_INNER, out_dtypes=(BF16,), epilogue=_ep_silu_gain,
               vecs=(row(l0_norm_g),), name="l0_proj_z")
    y = _attention(*(t.reshape(s3) for t in (q, k, lg_hi, lg_lo, v, zg)),
                   heads=HG_HEADS_PER_STEP, pack=2, per=8, dk=HG_DK, dv=HG_DK, name="hgrn2_attention")
    br = _oproj(y.reshape(m, D_INNER), l0_w_out, tm=OPROJ_TM, tn=OPROJ_TN, tk=OPROJ_TK,
                name="l0_oproj")
    h1, a1 = _ln_first(x, meta, br.reshape(bsz, T_TOT, D_MODEL), l0_ln_g, l0_ln_b, tr=LN_TR)
    a1 = a1.reshape(m, D_MODEL)

    proj1 = functools.partial(_proj, a1, l1_w_in, tm=PROJ_TM, tn=PROJ_TN)
    q = proj1(0, GLA_KW, out_dtypes=(BF16,),
              epilogue=functools.partial(_ep_scale, scale=GLA_DK ** -0.5), name="l1_proj_q")
    k = proj1(GLA_KW, GLA_KW, out_dtypes=(BF16,), name="l1_proj_k")
    v = proj1(2 * GLA_KW, D_INNER, out_dtypes=(BF16,), name="l1_proj_v")
    zg = proj1(2 * GLA_KW + D_INNER, D_INNER, out_dtypes=(BF16,), epilogue=_ep_silu_gain,
               vecs=(row(l1_norm_g),), name="l1_proj_z")
    w_g1p = jnp.pad(l1_w_g1, ((0, 0), (0, LANE - GLA_RANK)))
    w_g2p = jnp.pad(l1_w_g2, ((0, LANE - GLA_RANK), (0, 0)))
    g1 = _proj(a1, w_g1p, 0, LANE, tm=PROJ_TM, tn=LANE, out_dtypes=(BF16,), name="l1_proj_g1")
    lg_hi, lg_lo = _proj(g1, w_g2p, 0, GLA_KW, tm=PROJ_TM, tn=GATE_TN, out_dtypes=(BF16, BF16),
                         epilogue=_ep_gla_gate, vecs=(row(l1_b_g),), name="l1_gate")
    s2 = (bsz, T_TOT, GLA_KW)
    y = _attention(q.reshape(s2), k.reshape(s2), lg_hi.reshape(s2), lg_lo.reshape(s2),
                   v.reshape(s3), zg.reshape(s3),
                   heads=1, pack=1, per=8, dk=GLA_DK, dv=GLA_DV, name="gla_attention")
    br = _oproj(y.reshape(m, D_INNER), l1_w_out, tm=OPROJ_TM, tn=OPROJ_TN, tk=OPROJ_TK,
                name="l1_oproj")
    return _ln_seq_rows(h1, br.reshape(bsz, T_TOT, D_MODEL), l1_ln_g, l1_ln_b, tr=LN_TR,
                        name="l1_ln")
```

```python
import functools

import numpy as np
import jax
import jax.numpy as jnp
from jax import lax
from jax.experimental import pallas as pl
from jax.experimental.pallas import tpu as pltpu

D_MODEL = 4096
SEQ = 2048
N_META = 16
T_TOT = SEQ + N_META
D_INNER = 2 * D_MODEL
HG_DK = 128
GLA_HEADS = 8
GLA_KW = D_INNER // 2
GLA_DK = GLA_KW // GLA_HEADS
GLA_DV = D_INNER // GLA_HEADS
GLA_RANK = 16
GLA_GATE_NORM = 16.0
DEPTH = 2
ALPHA = (2.0 * DEPTH) ** 0.25
LN_EPS = 1e-5
RMS_EPS = 1e-6
LOG2_E = 1.4426950408889634

SUB = 16
HGRN2_BLK, GLA_BLK = 128, 256
LANE = 128
VMEM_LIMIT_V7X = 60 * 1024 * 1024

F32 = jnp.float32
BF16 = jnp.bfloat16


def _dot(a, b):
    return jnp.dot(a, b, preferred_element_type=F32)


def _dot_nt(a, b):
    return lax.dot_general(a, b, (((1,), (1,)), ((), ())), preferred_element_type=F32)


def _dot_tn(a, b):
    return lax.dot_general(a, b, (((0,), (0,)), ((), ())), preferred_element_type=F32)


def _sigmoid(x):
    return 0.5 + 0.5 * jnp.tanh(0.5 * x)


def _split_bf16(x):
    hi = x.astype(BF16)
    return hi, (x - hi.astype(F32)).astype(BF16)


def _ep_identity(acc):
    return (acc,)


def _ep_silu(acc):
    return (acc * _sigmoid(acc),)


def _ep_scale(acc, *, scale):
    return (acc * scale,)


def _ep_silu_gain(acc, gain):
    return (acc * _sigmoid(acc) * gain,)


def _ep_hgrn2_gate(acc, b_f, lb_logits):
    e = jnp.exp(lb_logits - jnp.max(lb_logits, axis=0, keepdims=True))
    lb = e[0:1, :] / jnp.sum(e, axis=0, keepdims=True)
    c = 0.5 * (1.0 - lb)
    ct = c * jnp.tanh(0.5 * (acc + b_f))
    hi, lo = _split_bf16(jnp.log2((lb + c) + ct))
    return hi, lo, c - ct


def _ep_gla_gate(acc, b_g):
    u = acc + b_g
    log_sig = jnp.minimum(u, 0.0) - jnp.log(1.0 + jnp.exp(-jnp.abs(u)))
    return _split_bf16(log_sig * (LOG2_E / GLA_GATE_NORM))


def _proj_kernel(a_ref, w_ref, *refs, epilogue, n_vec):
    vecs = [r[...] for r in refs[:n_vec]]
    acc = _dot(a_ref[...], w_ref[...].astype(BF16))
    for o_ref, val in zip(refs[n_vec:], epilogue(acc, *vecs)):
        o_ref[...] = val.astype(o_ref.dtype)


def _proj(a, w, col0, ncols, *, tm, tn, out_dtypes, epilogue=_ep_identity, vecs=(), name):
    m, k = a.shape
    assert m % tm == 0 and ncols % tn == 0 and col0 % tn == 0
    jb = col0 // tn
    outs = pl.pallas_call(
        functools.partial(_proj_kernel, epilogue=epilogue, n_vec=len(vecs)),
        out_shape=tuple(jax.ShapeDtypeStruct((m, ncols), dt) for dt in out_dtypes),
        grid=(m // tm, ncols // tn),
        in_specs=[
            pl.BlockSpec((tm, k), lambda i, j: (i, 0)),
            pl.BlockSpec((k, tn), lambda i, j: (0, j + jb)),
        ] + [pl.BlockSpec((vv.shape[0], tn), lambda i, j: (0, j)) for vv in vecs],
        out_specs=tuple(pl.BlockSpec((tm, tn), lambda i, j: (i, j)) for _ in out_dtypes),
        compiler_params=pltpu.CompilerParams(
            dimension_semantics=("arbitrary", "arbitrary"),
            vmem_limit_bytes=VMEM_LIMIT_V7X),
        name=name,
    )(a, w, *vecs)
    return outs if len(outs) > 1 else outs[0]


def _oproj_kernel(a_ref, w_ref, o_ref, acc_ref):
    kk = pl.program_id(2)

    @pl.when(kk == 0)
    def _():
        acc_ref[...] = jnp.zeros_like(acc_ref)

    acc_ref[...] += _dot(a_ref[...], w_ref[...].astype(BF16))

    @pl.when(kk == pl.num_programs(2) - 1)
    def _():
        o_ref[...] = acc_ref[...].astype(o_ref.dtype)


def _oproj(a, w, *, tm, tn, tk, name):
    m, k = a.shape
    n = w.shape[1]
    assert m % tm == 0 and n % tn == 0 and k % tk == 0
    return pl.pallas_call(
        _oproj_kernel,
        out_shape=jax.ShapeDtypeStruct((m, n), BF16),
        grid=(m // tm, n // tn, k // tk),
        in_specs=[
            pl.BlockSpec((tm, tk), lambda i, j, kk: (i, kk)),
            pl.BlockSpec((tk, tn), lambda i, j, kk: (kk, j)),
        ],
        out_specs=pl.BlockSpec((tm, tn), lambda i, j, kk: (i, j)),
        scratch_shapes=[pltpu.VMEM((tm, tn), F32)],
        compiler_params=pltpu.CompilerParams(
            dimension_semantics=("arbitrary", "arbitrary", "arbitrary"),
            vmem_limit_bytes=VMEM_LIMIT_V7X),
        name=name,
    )(a, w)


def _concat_cast_kernel(x_ref, meta_ref, o_ref):
    i = pl.program_id(1)

    @pl.when(i < pl.num_programs(1) - 1)
    def _():
        o_ref[...] = x_ref[...].astype(o_ref.dtype)

    @pl.when(i == pl.num_programs(1) - 1)
    def _():
        o_ref[0:N_META, :] = meta_ref[...].astype(o_ref.dtype)


def _concat_cast(x, meta, *, tr):
    bsz, _, d = x.shape
    n_x = SEQ // tr
    return pl.pallas_call(
        _concat_cast_kernel,
        out_shape=jax.ShapeDtypeStruct((bsz, T_TOT, d), BF16),
        grid=(bsz, n_x + 1),
        in_specs=[pl.BlockSpec((None, tr, d), lambda bi, i: (bi, jnp.minimum(i, n_x - 1), 0)),
                  pl.BlockSpec((N_META, d), lambda bi, i: (0, 0))],
        out_specs=pl.BlockSpec((None, tr, d), lambda bi, i: (bi, i, 0)),
        compiler_params=pltpu.CompilerParams(
            dimension_semantics=("arbitrary", "arbitrary"), vmem_limit_bytes=VMEM_LIMIT_V7X),
        name="concat_cast",
    )(x, meta)


def _ln_rows(h, br, g, b):
    r = ALPHA * h + br.astype(F32)
    mu = jnp.mean(r, axis=-1, keepdims=True)
    c = r - mu
    var = jnp.mean(c * c, axis=-1, keepdims=True)
    return c * lax.rsqrt(var + LN_EPS) * g + b


def _ln_first_kernel(x_ref, meta_ref, br_ref, g_ref, b_ref, o32_ref, o16_ref):
    i = pl.program_id(1)

    @pl.when(i < pl.num_programs(1) - 1)
    def _():
        y = _ln_rows(x_ref[...], br_ref[...], g_ref[...], b_ref[...])
        o32_ref[...] = y
        o16_ref[...] = y.astype(o16_ref.dtype)

    @pl.when(i == pl.num_programs(1) - 1)
    def _():
        y = _ln_rows(meta_ref[...], br_ref[0:N_META, :], g_ref[...], b_ref[...])
        o32_ref[0:N_META, :] = y
        o16_ref[0:N_META, :] = y.astype(o16_ref.dtype)


def _ln_first(x, meta, br, g, b, *, tr):
    bsz, _, d = x.shape
    n_x = SEQ // tr
    blk = pl.BlockSpec((None, tr, d), lambda bi, i: (bi, i, 0))
    vec = pl.BlockSpec((1, d), lambda bi, i: (0, 0))
    return pl.pallas_call(
        _ln_first_kernel,
        out_shape=(jax.ShapeDtypeStruct((bsz, T_TOT, d), F32),
                   jax.ShapeDtypeStruct((bsz, T_TOT, d), BF16)),
        grid=(bsz, n_x + 1),
        in_specs=[pl.BlockSpec((None, tr, d), lambda bi, i: (bi, jnp.minimum(i, n_x - 1), 0)),
                  pl.BlockSpec((N_META, d), lambda bi, i: (0, 0)), blk, vec, vec],
        out_specs=(blk, blk),
        compiler_params=pltpu.CompilerParams(
            dimension_semantics=("arbitrary", "arbitrary"), vmem_limit_bytes=VMEM_LIMIT_V7X),
        name="l0_ln",
    )(x, meta, br, g.reshape(1, d), b.reshape(1, d))


def _ln_kernel(h_ref, br_ref, g_ref, b_ref, o_ref):
    o_ref[...] = _ln_rows(h_ref[...], br_ref[...], g_ref[...], b_ref[...])


def _ln_seq_rows(h, br, g, b, *, tr, name):
    bsz, _, d = h.shape
    assert SEQ % tr == 0
    row = pl.BlockSpec((None, tr, d), lambda bi, i: (bi, i, 0))
    vec = pl.BlockSpec((1, d), lambda bi, i: (0, 0))
    return pl.pallas_call(
        _ln_kernel,
        out_shape=jax.ShapeDtypeStruct((bsz, SEQ, d), F32),
        grid=(bsz, SEQ // tr),
        in_specs=[row, row, vec, vec],
        out_specs=row,
        compiler_params=pltpu.CompilerParams(
            dimension_semantics=("arbitrary", "arbitrary"), vmem_limit_bytes=VMEM_LIMIT_V7X),
        name=name,
    )(h, br, g.reshape(1, d), b.reshape(1, d))


def _is_upper(p, g):
    return (p * SUB) % g >= g // 2


def _chunk_end_of_boundary(p, g):
    return ((p * SUB) // g * g + g // 2) // SUB - 1


def _row_scalar_table(n_sub, levels):
    pairs, index = [], {}

    def add(key, plus, minus):
        index[key] = len(pairs)
        pairs.append((plus, minus))

    last = n_sub - 1
    for p in range(1, n_sub):
        add(("q_dec", p), p - 1, None)
    for p in range(last):
        add(("k_dec", p), last, p)
    add(("g_last",), last, None)
    for g in levels:
        for p in range(n_sub):
            m = _chunk_end_of_boundary(p, g)
            if _is_upper(p, g) and m != p - 1:
                add(("up", g, p), p - 1, m)
            if not _is_upper(p, g) and m != p:
                add(("lo", g, p), m, p)
    return index, pairs


class _BlockPlan:
    def __init__(self, blk):
        assert blk % (2 * SUB) == 0 and blk & (blk - 1) == 0
        self.blk = blk
        self.n_sub = blk // SUB
        self.levels = tuple(2 * SUB << i for i in range((blk // (2 * SUB)).bit_length()))
        self.scalar_row, self.pairs = _row_scalar_table(self.n_sub, self.levels)
        self.n_scalar_rows = -(-len(self.pairs) // SUB) * SUB

    def consts(self):
        c = self.blk
        i = np.arange(c)[:, None]
        s = np.arange(c)[None, :]
        summ = np.zeros((c + self.n_scalar_rows, c), np.float32)
        summ[:c] = s <= i
        for r, (plus, minus) in enumerate(self.pairs):
            if plus is not None:
                summ[c + r, :SUB * (plus + 1)] += 1.0
            if minus is not None:
                summ[c + r, :SUB * (minus + 1)] -= 1.0
        lvl = np.full((c, c), -1, np.int32)
        lvl[((i // SUB) == (s // SUB)) & (s <= i)] = 0
        for l, g in enumerate(self.levels):
            half = g // 2
            lvl[((i // g) == (s // g)) & ((i % g) >= half) & ((s % g) < half)] = l + 1
        return jnp.asarray(np.concatenate([summ, summ], axis=1), BF16), jnp.asarray(lvl)


def _pieces(x):
    return [x[p * SUB:(p + 1) * SUB, :] for p in range(x.shape[0] // SUB)]


def _rows(b, r, n):
    return jnp.broadcast_to(b[r:r + 1, :], (n, b.shape[1]))


def _scaled_operands(q, k, cum, plan):
    n = q.shape[1]
    n_sub, blk = plan.n_sub, plan.blk
    b = cum[:blk]
    b_pc = _pieces(b)
    ends = [_rows(b, (p + 1) * SUB - 1, SUB) for p in range(n_sub)]
    e0 = jnp.concatenate([b_pc[0]] + [b_pc[p] - ends[p - 1] for p in range(1, n_sub)], axis=0)
    to_end = jnp.concatenate([ends[p] - b_pc[p] for p in range(n_sub)], axis=0)
    scal = jnp.exp2(cum[blk:])

    def factor(key):
        r = plan.scalar_row[key]
        return jnp.broadcast_to(scal[r:r + 1, :], (SUB, n)).astype(BF16)

    def scaled(piece, key):
        return piece * factor(key) if key in plan.scalar_row else piece

    q0 = q * jnp.exp2(e0).astype(BF16)
    k0 = k * jnp.exp2(-e0).astype(BF16)
    ky = k * jnp.exp2(to_end).astype(BF16)
    q0_pc, k_pc, ky_pc = _pieces(q0), _pieces(k), _pieces(ky)
    ops = {"q0": q0, "k0": k0, "q_up": [], "k_mix": []}
    for g in plan.levels:
        ops["q_up"].append(jnp.concatenate(
            [scaled(q0_pc[p], ("up", g, p)) for p in range(n_sub) if _is_upper(p, g)], axis=0))
        ops["k_mix"].append(jnp.concatenate(
            [k_pc[p] if _is_upper(p, g) else scaled(ky_pc[p], ("lo", g, p))
             for p in range(n_sub)], axis=0))
    ops["q_dec"] = jnp.concatenate([scaled(q0_pc[p], ("q_dec", p)) for p in range(n_sub)], axis=0)
    ops["k_dec"] = jnp.concatenate([scaled(ky_pc[p], ("k_dec", p)) for p in range(n_sub)], axis=0)
    r = plan.scalar_row[("g_last",)]
    ops["g_last"] = scal[r:r + 1, :]
    return ops


def _masked_scores(raw, lvl, lvl_pc, plan):
    sc_pc = _pieces(jnp.where(lvl == 0, raw[0], 0.0))
    for l, g in enumerate(plan.levels):
        ups = [p for p in range(plan.n_sub) if _is_upper(p, g)]
        for j, p in enumerate(ups):
            sc_pc[p] = jnp.where(lvl_pc[p] == l + 1, raw[l + 1][j * SUB:(j + 1) * SUB, :], sc_pc[p])
    return jnp.concatenate(sc_pc, axis=0).astype(BF16)


def _norm_gate(o, zg):
    ms = jnp.mean(o * o, axis=-1, keepdims=True)
    return o * lax.rsqrt(ms + RMS_EPS) * zg


def _attn_kernel(q_ref, k_ref, lh_ref, ll_ref, v_ref, zg_ref, sum_ref, lvl_ref, y_ref, st_ref,
                 st_meta_ref, y_meta_ref, *, heads, pack, per, plan):
    blk = plan.blk
    lvl = lvl_ref[...]
    lvl_pc = _pieces(lvl)
    summ = sum_ref[...]
    dk = q_ref.shape[1] // heads
    dv = v_ref.shape[1] // heads
    units = range(heads // pack)
    members = range(pack)
    ksl = [slice(u * pack * dk, (u + 1) * pack * dk) for u in units]
    vsl = [slice(u * pack * dv, (u + 1) * pack * dv) for u in units]
    kin = [slice(p * dk, (p + 1) * dk) for p in members]
    vin = [slice(p * dv, (p + 1) * dv) for p in members]

    def block_diag(x):
        if pack == 1:
            return x
        zero = jnp.zeros((x.shape[0], dk), x.dtype)
        return jnp.concatenate(
            [jnp.concatenate([x[:, kin[p]] if c == p else zero for c in members], axis=1)
             for p in members], axis=0)

    def pad_rows(t):
        return jnp.concatenate([t, jnp.zeros((blk - N_META, t.shape[1]), t.dtype)], axis=0)

    def meta_block():
        rows = pl.ds(SEQ, N_META)
        b = _dot(summ[0:N_META, :], jnp.concatenate(
            [pad_rows(lh_ref[rows, :]), pad_rows(ll_ref[rows, :])], axis=0))
        q, k = q_ref[rows, :], k_ref[rows, :]
        q0 = q * jnp.exp2(b).astype(BF16)
        k0 = pad_rows(k * jnp.exp2(-b).astype(BF16))
        k_dec = pad_rows(k * jnp.exp2(b[N_META - 1:N_META, :] - b).astype(BF16))
        v = pad_rows(v_ref[rows, :])
        for u in units:
            for p in members:
                kh = slice(ksl[u].start + p * dk, ksl[u].start + (p + 1) * dk)
                vh = slice(vsl[u].start + p * dv, vsl[u].start + (p + 1) * dv)
                sc = jnp.where(lvl[0:N_META, :] == 0, _dot_nt(q0[:, kh], k0[:, kh]), 0.0)
                o = _dot(sc.astype(BF16), v[:, vh])
                st_ref[u, :, vin[p]] = _dot_tn(k_dec[:, kh], v[:, vh])
                y_ref[rows, vh] = _norm_gate(o, zg_ref[rows, vh].astype(F32)).astype(y_ref.dtype)

    def blocks(starts):
        nb = len(starts)
        grp_a, grp_b = range(nb // 2), range(nb // 2, nb)
        rows = [pl.ds(r0, blk) for r0 in starts]
        cum = [_dot(summ, jnp.concatenate([lh_ref[r, :], ll_ref[r, :]], axis=0)) for r in rows]
        ops = [[None] * len(units) for _ in range(nb)]
        raw = [[None] * len(units) for _ in range(nb)]
        sc = [[None] * len(units) for _ in range(nb)]
        upd = [[None] * len(units) for _ in range(nb)]
        o = [[None] * len(units) for _ in range(nb)]

        def prepare(n, u):
            ops[n][u] = _scaled_operands(q_ref[rows[n], ksl[u]], k_ref[rows[n], ksl[u]],
                                         cum[n][:, ksl[u]], plan)

        def score(n, u):
            op = ops[n][u]
            raw[n][u] = [_dot_nt(block_diag(op["q0"]), op["k0"])] + [
                _dot_nt(block_diag(op["q_up"][l]), op["k_mix"][l]) for l in range(len(plan.levels))]

        def mask_and_update(n, u):
            sc[n][u] = [_masked_scores(
                [r[p * (r.shape[0] // pack):(p + 1) * (r.shape[0] // pack), :] for r in raw[n][u]],
                lvl, lvl_pc, plan) for p in members]
            upd[n][u] = _dot_tn(ops[n][u]["k_dec"], v_ref[rows[n], vsl[u]])

        def output(n, u):
            st = st_ref[u]
            lhs = jnp.concatenate(
                [jnp.concatenate([sc[n][u][p], ops[n][u]["q_dec"][:, kin[p]]], axis=1)
                 for p in members], axis=0)
            res = _dot(lhs, jnp.concatenate([v_ref[rows[n], vsl[u]], st.astype(BF16)], axis=0))
            o[n][u] = [res[p * blk:(p + 1) * blk, vin[p]] for p in members]
            for p in members:
                g_col = jnp.transpose(ops[n][u]["g_last"][:, kin[p]])
                st_ref[u, :, vin[p]] = g_col * st[:, vin[p]] + upd[n][u][kin[p], vin[p]]

        def finish(n, u):
            for p in members:
                vh = slice(vsl[u].start + p * dv, vsl[u].start + (p + 1) * dv)
                y = _norm_gate(o[n][u][p], zg_ref[rows[n], vh].astype(F32))
                y_ref[rows[n], vh] = y.astype(y_ref.dtype)

        for n in grp_a:
            for u in units:
                prepare(n, u)
        for n, m in zip(grp_a, grp_b):
            for u in units:
                score(n, u)
                prepare(m, u)
        for n in grp_a:
            for u in units:
                mask_and_update(n, u)
        for n, m in zip(grp_a, grp_b):
            for u in units:
                output(n, u)
                score(m, u)
        for n, m in zip(grp_a, grp_b):
            for u in units:
                finish(n, u)
                mask_and_update(m, u)
        for m in grp_b:
            for u in units:
                output(m, u)
        for m in grp_b:
            for u in units:
                finish(m, u)

    meta_rows = pl.ds(SEQ, N_META)

    @pl.when(pl.program_id(1) == 0)
    def _():
        meta_block()
        st_meta_ref[...] = st_ref[...]
        y_meta_ref[...] = y_ref[meta_rows, :]

    @pl.when(pl.program_id(1) > 0)
    def _():
        st_ref[...] = st_meta_ref[...]
        y_ref[meta_rows, :] = y_meta_ref[...]

    def step(i, carry):
        base = pl.multiple_of(i * (per * blk), per * blk)
        blocks([base + n * blk for n in range(per)])
        return carry

    lax.fori_loop(0, SEQ // (per * blk), step, 0)


def _attention(q, k, lg_hi, lg_lo, v, zg, *, heads, pack, per, blk, dk, dv, name):
    assert heads % pack == 0
    assert SEQ % (per * blk) == 0
    plan = _BlockPlan(blk)
    bsz, t, _ = q.shape
    n_groups = v.shape[2] // (heads * dv)
    summ, lvl = plan.consts()
    qk = pl.BlockSpec((None, t, heads * dk), lambda g, bi: (bi, 0, g))
    vz = pl.BlockSpec((None, t, heads * dv), lambda g, bi: (bi, 0, g))
    return pl.pallas_call(
        functools.partial(_attn_kernel, heads=heads, pack=pack, per=per, plan=plan),
        out_shape=jax.ShapeDtypeStruct(v.shape, BF16),
        grid=(n_groups, bsz),
        in_specs=[qk, qk, qk, qk, vz, vz,
                  pl.BlockSpec(summ.shape, lambda g, bi: (0, 0)),
                  pl.BlockSpec(lvl.shape, lambda g, bi: (0, 0))],
        out_specs=vz,
        scratch_shapes=[pltpu.VMEM((heads // pack, dk, pack * dv), F32),
                        pltpu.VMEM((heads // pack, dk, pack * dv), F32),
                        pltpu.VMEM((N_META, heads * dv), BF16)],
        compiler_params=pltpu.CompilerParams(
            dimension_semantics=("arbitrary", "arbitrary"), vmem_limit_bytes=VMEM_LIMIT_V7X),
        name=name,
    )(q, k, lg_hi, lg_lo, v, zg, summ, lvl)


PROJ_TM = 1376
PROJ_TN = 512
OPROJ_TM, OPROJ_TN, OPROJ_TK = 2064, 1024, 1024
LN_TR = 512
LN0_TR = 256
GATE_TN = 1024
HG_HEADS_PER_STEP = 4


def kernel(x, meta, lb_logits, l0_w_in, l0_b_f, l0_norm_g, l0_w_out, l0_ln_g, l0_ln_b,
           l1_w_in, l1_w_g1, l1_w_g2, l1_b_g, l1_norm_g, l1_w_out, l1_ln_g, l1_ln_b):
    bsz = x.shape[0]
    m = bsz * T_TOT
    a0 = _concat_cast(x, meta, tr=LN_TR).reshape(m, D_MODEL)
    s3 = (bsz, T_TOT, D_INNER)
    row = lambda vv: vv.reshape(1, -1)

    proj0 = functools.partial(_proj, a0, l0_w_in, tm=PROJ_TM, tn=PROJ_TN)
    q = proj0(0 * D_INNER, D_INNER, out_dtypes=(BF16,), epilogue=_ep_silu, name="l0_proj_q")
    lg_hi, lg_lo, k = proj0(1 * D_INNER, D_INNER, out_dtypes=(BF16, BF16, BF16),
                            epilogue=_ep_hgrn2_gate, vecs=(row(l0_b_f), lb_logits),
                            name="l0_proj_f")
    v = proj0(2 * D_INNER, D_INNER, out_dtypes=(BF16,), name="l0_proj_i")
    zg = proj0(3 * D_INNER, D_INNER, out_dtypes=(BF16,), epilogue=_ep_silu_gain,
               vecs=(row(l0_norm_g),), name="l0_proj_z")
    y = _attention(*(t.reshape(s3) for t in (q, k, lg_hi, lg_lo, v, zg)),
                   heads=HG_HEADS_PER_STEP, pack=2, per=8, blk=HGRN2_BLK, dk=HG_DK, dv=HG_DK, name="hgrn2_attention")
    br = _oproj(y.reshape(m, D_INNER), l0_w_out, tm=OPROJ_TM, tn=OPROJ_TN, tk=OPROJ_TK,
                name="l0_oproj")
    h1, a1 = _ln_first(x, meta, br.reshape(bsz, T_TOT, D_MODEL), l0_ln_g, l0_ln_b, tr=LN0_TR)
    a1 = a1.reshape(m, D_MODEL)

    proj1 = functools.partial(_proj, a1, l1_w_in, tm=PROJ_TM, tn=PROJ_TN)
    q = proj1(0, GLA_KW, out_dtypes=(BF16,),
              epilogue=functools.partial(_ep_scale, scale=GLA_DK ** -0.5), name="l1_proj_q")
    k = proj1(GLA_KW, GLA_KW, out_dtypes=(BF16,), name="l1_proj_k")
    v = proj1(2 * GLA_KW, D_INNER, out_dtypes=(BF16,), name="l1_proj_v")
    zg = proj1(2 * GLA_KW + D_INNER, D_INNER, out_dtypes=(BF16,), epilogue=_ep_silu_gain,
               vecs=(row(l1_norm_g),), name="l1_proj_z")
    w_g1p = jnp.pad(l1_w_g1, ((0, 0), (0, LANE - GLA_RANK)))
    w_g2p = jnp.pad(l1_w_g2, ((0, LANE - GLA_RANK), (0, 0)))
    g1 = _proj(a1, w_g1p, 0, LANE, tm=PROJ_TM, tn=LANE, out_dtypes=(BF16,), name="l1_proj_g1")
    lg_hi, lg_lo = _proj(g1, w_g2p, 0, GLA_KW, tm=PROJ_TM, tn=GATE_TN, out_dtypes=(BF16, BF16),
                         epilogue=_ep_gla_gate, vecs=(row(l1_b_g),), name="l1_gate")
    s2 = (bsz, T_TOT, GLA_KW)
    y = _attention(q.reshape(s2), k.reshape(s2), lg_hi.reshape(s2), lg_lo.reshape(s2),
                   v.reshape(s3), zg.reshape(s3),
                   heads=1, pack=1, per=4, blk=GLA_BLK, dk=GLA_DK, dv=GLA_DV, name="gla_attention")
    br = _oproj(y.reshape(m, D_INNER), l1_w_out, tm=OPROJ_TM, tn=OPROJ_TN, tk=OPROJ_TK,
                name="l1_oproj")
    return _ln_seq_rows(h1, br.reshape(bsz, T_TOT, D_MODEL), l1_ln_g, l1_ln_b, tr=LN_TR,
                        name="l1_ln")
```

```python
import functools

import numpy as np
import jax
import jax.numpy as jnp
from jax import lax
from jax.experimental import pallas as pl
from jax.experimental.pallas import tpu as pltpu

D_MODEL = 4096
SEQ = 2048
N_META = 16
T_TOT = SEQ + N_META
D_INNER = 2 * D_MODEL
HG_DK = 128
GLA_HEADS = 8
GLA_KW = D_INNER // 2
GLA_DK = GLA_KW // GLA_HEADS
GLA_DV = D_INNER // GLA_HEADS
GLA_RANK = 16
GLA_GATE_NORM = 16.0
DEPTH = 2
ALPHA = (2.0 * DEPTH) ** 0.25
LN_EPS = 1e-5
RMS_EPS = 1e-6
LOG2_E = 1.4426950408889634

SUB = 16
HGRN2_BLK, GLA_BLK = 128, 256
LANE = 128
VMEM_LIMIT_V7X = 60 * 1024 * 1024

F32 = jnp.float32
BF16 = jnp.bfloat16


def _dot(a, b):
    return jnp.dot(a, b, preferred_element_type=F32)


def _dot_nt(a, b):
    return lax.dot_general(a, b, (((1,), (1,)), ((), ())), preferred_element_type=F32)


def _dot_tn(a, b):
    return lax.dot_general(a, b, (((0,), (0,)), ((), ())), preferred_element_type=F32)


def _sigmoid(x):
    return 0.5 + 0.5 * jnp.tanh(0.5 * x)


def _split_bf16(x):
    hi = x.astype(BF16)
    return hi, (x - hi.astype(F32)).astype(BF16)


def _ep_identity(acc):
    return (acc,)


def _ep_silu(acc):
    return (acc * _sigmoid(acc),)


def _ep_scale(acc, *, scale):
    return (acc * scale,)


def _ep_silu_gain(acc, gain):
    return (acc * _sigmoid(acc) * gain,)


def _ep_hgrn2_gate(acc, b_f, lb_logits):
    e = jnp.exp(lb_logits - jnp.max(lb_logits, axis=0, keepdims=True))
    lb = e[0:1, :] / jnp.sum(e, axis=0, keepdims=True)
    c = 0.5 * (1.0 - lb)
    ct = c * jnp.tanh(0.5 * (acc + b_f))
    hi, lo = _split_bf16(jnp.log2((lb + c) + ct))
    return hi, lo, c - ct


def _ep_gla_gate(acc, b_g):
    u = acc + b_g
    log_sig = jnp.minimum(u, 0.0) - jnp.log(1.0 + jnp.exp(-jnp.abs(u)))
    return _split_bf16(log_sig * (LOG2_E / GLA_GATE_NORM))


def _proj_kernel(a_ref, w_ref, *refs, epilogue, n_vec):
    vecs = [r[...] for r in refs[:n_vec]]
    acc = _dot(a_ref[...], w_ref[...].astype(BF16))
    for o_ref, val in zip(refs[n_vec:], epilogue(acc, *vecs)):
        o_ref[...] = val.astype(o_ref.dtype)


def _proj(a, w, col0, ncols, *, tm, tn, out_dtypes, epilogue=_ep_identity, vecs=(), name):
    m, k = a.shape
    assert m % tm == 0 and ncols % tn == 0 and col0 % tn == 0
    jb = col0 // tn
    outs = pl.pallas_call(
        functools.partial(_proj_kernel, epilogue=epilogue, n_vec=len(vecs)),
        out_shape=tuple(jax.ShapeDtypeStruct((m, ncols), dt) for dt in out_dtypes),
        grid=(m // tm, ncols // tn),
        in_specs=[
            pl.BlockSpec((tm, k), lambda i, j: (i, 0)),
            pl.BlockSpec((k, tn), lambda i, j: (0, j + jb)),
        ] + [pl.BlockSpec((vv.shape[0], tn), lambda i, j: (0, j)) for vv in vecs],
        out_specs=tuple(pl.BlockSpec((tm, tn), lambda i, j: (i, j)) for _ in out_dtypes),
        compiler_params=pltpu.CompilerParams(
            dimension_semantics=("arbitrary", "arbitrary"),
            vmem_limit_bytes=VMEM_LIMIT_V7X),
        name=name,
    )(a, w, *vecs)
    return outs if len(outs) > 1 else outs[0]


def _gla_gate_kernel(a_ref, w1_ref, w2_ref, b_ref, hi_ref, lo_ref, g1_ref):
    @pl.when(pl.program_id(1) == 0)
    def _():
        g1_ref[...] = _dot(a_ref[...], w1_ref[...].astype(BF16)).astype(g1_ref.dtype)

    hi, lo = _ep_gla_gate(_dot(g1_ref[...], w2_ref[...].astype(BF16)), b_ref[...])
    hi_ref[...] = hi
    lo_ref[...] = lo


def _gla_gate(a, w1, w2, b, *, tm, tn):
    m, k = a.shape
    r, n = w2.shape
    assert m % tm == 0 and n % tn == 0 and w1.shape == (k, r)
    out = pl.BlockSpec((tm, tn), lambda i, j: (i, j))
    return pl.pallas_call(
        _gla_gate_kernel,
        out_shape=(jax.ShapeDtypeStruct((m, n), BF16), jax.ShapeDtypeStruct((m, n), BF16)),
        grid=(m // tm, n // tn),
        in_specs=[pl.BlockSpec((tm, k), lambda i, j: (i, 0)),
                  pl.BlockSpec((k, r), lambda i, j: (0, 0)),
                  pl.BlockSpec((r, tn), lambda i, j: (0, j)),
                  pl.BlockSpec((1, tn), lambda i, j: (0, j))],
        out_specs=(out, out),
        scratch_shapes=[pltpu.VMEM((tm, r), BF16)],
        compiler_params=pltpu.CompilerParams(
            dimension_semantics=("arbitrary", "arbitrary"), vmem_limit_bytes=VMEM_LIMIT_V7X),
        name="l1_gate",
    )(a, w1, w2, b)


def _oproj_kernel(a_ref, w_ref, o_ref, acc_ref):
    kk = pl.program_id(2)

    @pl.when(kk == 0)
    def _():
        acc_ref[...] = _dot(a_ref[...], w_ref[...].astype(BF16))

    @pl.when(kk > 0)
    def _():
        acc_ref[...] += _dot(a_ref[...], w_ref[...].astype(BF16))

    @pl.when(kk == pl.num_programs(2) - 1)
    def _():
        o_ref[...] = acc_ref[...].astype(o_ref.dtype)


def _oproj(a, w, *, tm, tn, tk, name):
    m, k = a.shape
    n = w.shape[1]
    assert m % tm == 0 and n % tn == 0 and k % tk == 0
    return pl.pallas_call(
        _oproj_kernel,
        out_shape=jax.ShapeDtypeStruct((m, n), BF16),
        grid=(m // tm, n // tn, k // tk),
        in_specs=[
            pl.BlockSpec((tm, tk), lambda i, j, kk: (i, kk)),
            pl.BlockSpec((tk, tn), lambda i, j, kk: (kk, j)),
        ],
        out_specs=pl.BlockSpec((tm, tn), lambda i, j, kk: (i, j)),
        scratch_shapes=[pltpu.VMEM((tm, tn), F32)],
        compiler_params=pltpu.CompilerParams(
            dimension_semantics=("arbitrary", "arbitrary", "arbitrary"),
            vmem_limit_bytes=VMEM_LIMIT_V7X),
        name=name,
    )(a, w)


def _concat_cast_kernel(x_ref, meta_ref, o_ref):
    i = pl.program_id(1)

    @pl.when(i < pl.num_programs(1) - 1)
    def _():
        o_ref[...] = x_ref[...].astype(o_ref.dtype)

    @pl.when(i == pl.num_programs(1) - 1)
    def _():
        o_ref[0:N_META, :] = meta_ref[...].astype(o_ref.dtype)


def _concat_cast(x, meta, *, tr):
    bsz, _, d = x.shape
    n_x = SEQ // tr
    return pl.pallas_call(
        _concat_cast_kernel,
        out_shape=jax.ShapeDtypeStruct((bsz, T_TOT, d), BF16),
        grid=(bsz, n_x + 1),
        in_specs=[pl.BlockSpec((None, tr, d), lambda bi, i: (bi, jnp.minimum(i, n_x - 1), 0)),
                  pl.BlockSpec((N_META, d), lambda bi, i: (0, 0))],
        out_specs=pl.BlockSpec((None, tr, d), lambda bi, i: (bi, i, 0)),
        compiler_params=pltpu.CompilerParams(
            dimension_semantics=("arbitrary", "arbitrary"), vmem_limit_bytes=VMEM_LIMIT_V7X),
        name="concat_cast",
    )(x, meta)


def _ln_rows(h, br, g, b):
    r = ALPHA * h + br.astype(F32)
    mu = jnp.mean(r, axis=-1, keepdims=True)
    c = r - mu
    var = jnp.mean(c * c, axis=-1, keepdims=True)
    return c * lax.rsqrt(var + LN_EPS) * g + b


def _ln_first_kernel(x_ref, meta_ref, br_ref, g_ref, b_ref, o32_ref, o16_ref):
    i = pl.program_id(1)

    @pl.when(i < pl.num_programs(1) - 1)
    def _():
        y = _ln_rows(x_ref[...], br_ref[...], g_ref[...], b_ref[...])
        o32_ref[...] = y
        o16_ref[...] = y.astype(o16_ref.dtype)

    @pl.when(i == pl.num_programs(1) - 1)
    def _():
        y = _ln_rows(meta_ref[...], br_ref[0:N_META, :], g_ref[...], b_ref[...])
        o32_ref[0:N_META, :] = y
        o16_ref[0:N_META, :] = y.astype(o16_ref.dtype)


def _ln_first(x, meta, br, g, b, *, tr):
    bsz, _, d = x.shape
    n_x = SEQ // tr
    blk = pl.BlockSpec((None, tr, d), lambda bi, i: (bi, i, 0))
    vec = pl.BlockSpec((1, d), lambda bi, i: (0, 0))
    return pl.pallas_call(
        _ln_first_kernel,
        out_shape=(jax.ShapeDtypeStruct((bsz, T_TOT, d), F32),
                   jax.ShapeDtypeStruct((bsz, T_TOT, d), BF16)),
        grid=(bsz, n_x + 1),
        in_specs=[pl.BlockSpec((None, tr, d), lambda bi, i: (bi, jnp.minimum(i, n_x - 1), 0)),
                  pl.BlockSpec((N_META, d), lambda bi, i: (0, 0)), blk, vec, vec],
        out_specs=(blk, blk),
        compiler_params=pltpu.CompilerParams(
            dimension_semantics=("arbitrary", "arbitrary"), vmem_limit_bytes=VMEM_LIMIT_V7X),
        name="l0_ln",
    )(x, meta, br, g.reshape(1, d), b.reshape(1, d))


def _ln_kernel(h_ref, br_ref, g_ref, b_ref, o_ref):
    o_ref[...] = _ln_rows(h_ref[...], br_ref[...], g_ref[...], b_ref[...])


def _ln_seq_rows(h, br, g, b, *, tr, name):
    bsz, _, d = h.shape
    assert SEQ % tr == 0
    row = pl.BlockSpec((None, tr, d), lambda bi, i: (bi, i, 0))
    vec = pl.BlockSpec((1, d), lambda bi, i: (0, 0))
    return pl.pallas_call(
        _ln_kernel,
        out_shape=jax.ShapeDtypeStruct((bsz, SEQ, d), F32),
        grid=(bsz, SEQ // tr),
        in_specs=[row, row, vec, vec],
        out_specs=row,
        compiler_params=pltpu.CompilerParams(
            dimension_semantics=("arbitrary", "arbitrary"), vmem_limit_bytes=VMEM_LIMIT_V7X),
        name=name,
    )(h, br, g.reshape(1, d), b.reshape(1, d))


def _is_upper(p, g):
    return (p * SUB) % g >= g // 2


def _chunk_end_of_boundary(p, g):
    return ((p * SUB) // g * g + g // 2) // SUB - 1


def _row_scalar_table(n_sub, levels):
    pairs, index = [], {}

    def add(key, plus, minus):
        index[key] = len(pairs)
        pairs.append((plus, minus))

    last = n_sub - 1
    for p in range(1, n_sub):
        add(("q_dec", p), p - 1, None)
    for p in range(last):
        add(("k_dec", p), last, p)
    add(("g_last",), last, None)
    for g in levels:
        for p in range(n_sub):
            m = _chunk_end_of_boundary(p, g)
            if _is_upper(p, g) and m != p - 1:
                add(("up", g, p), p - 1, m)
            if not _is_upper(p, g) and m != p:
                add(("lo", g, p), m, p)
    return index, pairs


class _BlockPlan:
    def __init__(self, blk):
        assert blk % (2 * SUB) == 0 and blk & (blk - 1) == 0
        self.blk = blk
        self.n_sub = blk // SUB
        self.levels = tuple(2 * SUB << i for i in range((blk // (2 * SUB)).bit_length()))
        self.scalar_row, self.pairs = _row_scalar_table(self.n_sub, self.levels)
        self.n_scalar_rows = -(-len(self.pairs) // SUB) * SUB

    def consts(self):
        c = self.blk
        i = np.arange(c)[:, None]
        s = np.arange(c)[None, :]
        summ = np.zeros((c + self.n_scalar_rows, c), np.float32)
        summ[:c] = s <= i
        for r, (plus, minus) in enumerate(self.pairs):
            if plus is not None:
                summ[c + r, :SUB * (plus + 1)] += 1.0
            if minus is not None:
                summ[c + r, :SUB * (minus + 1)] -= 1.0
        lvl = np.full((c, c), -1, np.int32)
        lvl[((i // SUB) == (s // SUB)) & (s <= i)] = 0
        for l, g in enumerate(self.levels):
            half = g // 2
            lvl[((i // g) == (s // g)) & ((i % g) >= half) & ((s % g) < half)] = l + 1
        return jnp.asarray(np.concatenate([summ, summ], axis=1), BF16), jnp.asarray(lvl)


def _pieces(x):
    return [x[p * SUB:(p + 1) * SUB, :] for p in range(x.shape[0] // SUB)]


def _rows(b, r, n):
    return jnp.broadcast_to(b[r:r + 1, :], (n, b.shape[1]))


def _scaled_operands(q, k, cum, plan):
    n = q.shape[1]
    n_sub, blk = plan.n_sub, plan.blk
    b = cum[:blk]
    b_pc = _pieces(b)
    ends = [_rows(b, (p + 1) * SUB - 1, SUB) for p in range(n_sub)]
    e0 = jnp.concatenate([b_pc[0]] + [b_pc[p] - ends[p - 1] for p in range(1, n_sub)], axis=0)
    to_end = jnp.concatenate([ends[p] - b_pc[p] for p in range(n_sub)], axis=0)
    scal = jnp.exp2(cum[blk:])

    def factor(key):
        r = plan.scalar_row[key]
        return jnp.broadcast_to(scal[r:r + 1, :], (SUB, n)).astype(BF16)

    def scaled(piece, key):
        return piece * factor(key) if key in plan.scalar_row else piece

    q0 = q * jnp.exp2(e0).astype(BF16)
    k0 = k * jnp.exp2(-e0).astype(BF16)
    ky = k * jnp.exp2(to_end).astype(BF16)
    q0_pc, k_pc, ky_pc = _pieces(q0), _pieces(k), _pieces(ky)
    ops = {"q0": q0, "k0": k0, "q_up": [], "k_mix": []}
    for g in plan.levels:
        ops["q_up"].append(jnp.concatenate(
            [scaled(q0_pc[p], ("up", g, p)) for p in range(n_sub) if _is_upper(p, g)], axis=0))
        ops["k_mix"].append(jnp.concatenate(
            [k_pc[p] if _is_upper(p, g) else scaled(ky_pc[p], ("lo", g, p))
             for p in range(n_sub)], axis=0))
    ops["q_dec"] = jnp.concatenate([scaled(q0_pc[p], ("q_dec", p)) for p in range(n_sub)], axis=0)
    ops["k_dec"] = jnp.concatenate([scaled(ky_pc[p], ("k_dec", p)) for p in range(n_sub)], axis=0)
    r = plan.scalar_row[("g_last",)]
    ops["g_last"] = scal[r:r + 1, :]
    return ops


def _masked_scores(raw, lvl, lvl_pc, plan):
    sc_pc = _pieces(jnp.where(lvl == 0, raw[0], 0.0))
    for l, g in enumerate(plan.levels):
        ups = [p for p in range(plan.n_sub) if _is_upper(p, g)]
        for j, p in enumerate(ups):
            sc_pc[p] = jnp.where(lvl_pc[p] == l + 1, raw[l + 1][j * SUB:(j + 1) * SUB, :], sc_pc[p])
    return jnp.concatenate(sc_pc, axis=0).astype(BF16)


def _norm_gate(o, zg):
    ms = jnp.mean(o * o, axis=-1, keepdims=True)
    return o * lax.rsqrt(ms + RMS_EPS) * zg


def _attn_kernel(q_ref, k_ref, lh_ref, ll_ref, v_ref, zg_ref, sum_ref, lvl_ref, y_ref, st_ref,
                 st_meta_ref, y_meta_ref, *, heads, pack, per, plan):
    blk = plan.blk
    lvl = lvl_ref[...]
    lvl_pc = _pieces(lvl)
    summ = sum_ref[...]
    dk = q_ref.shape[1] // heads
    dv = v_ref.shape[1] // heads
    units = range(heads // pack)
    members = range(pack)
    ksl = [slice(u * pack * dk, (u + 1) * pack * dk) for u in units]
    vsl = [slice(u * pack * dv, (u + 1) * pack * dv) for u in units]
    kin = [slice(p * dk, (p + 1) * dk) for p in members]
    vin = [slice(p * dv, (p + 1) * dv) for p in members]

    def block_diag(x):
        if pack == 1:
            return x
        zero = jnp.zeros((x.shape[0], dk), x.dtype)
        return jnp.concatenate(
            [jnp.concatenate([x[:, kin[p]] if c == p else zero for c in members], axis=1)
             for p in members], axis=0)

    def pad_rows(t):
        return jnp.concatenate([t, jnp.zeros((blk - N_META, t.shape[1]), t.dtype)], axis=0)

    def meta_block():
        rows = pl.ds(SEQ, N_META)
        b = _dot(summ[0:N_META, :], jnp.concatenate(
            [pad_rows(lh_ref[rows, :]), pad_rows(ll_ref[rows, :])], axis=0))
        q, k = q_ref[rows, :], k_ref[rows, :]
        q0 = q * jnp.exp2(b).astype(BF16)
        k0 = pad_rows(k * jnp.exp2(-b).astype(BF16))
        k_dec = pad_rows(k * jnp.exp2(b[N_META - 1:N_META, :] - b).astype(BF16))
        v = pad_rows(v_ref[rows, :])
        for u in units:
            for p in members:
                kh = slice(ksl[u].start + p * dk, ksl[u].start + (p + 1) * dk)
                vh = slice(vsl[u].start + p * dv, vsl[u].start + (p + 1) * dv)
                sc = jnp.where(lvl[0:N_META, :] == 0, _dot_nt(q0[:, kh], k0[:, kh]), 0.0)
                o = _dot(sc.astype(BF16), v[:, vh])
                st_ref[u, :, vin[p]] = _dot_tn(k_dec[:, kh], v[:, vh])
                y_ref[rows, vh] = _norm_gate(o, zg_ref[rows, vh].astype(F32)).astype(y_ref.dtype)

    def blocks(starts):
        nb = len(starts)
        grp_a, grp_b = range(nb // 2), range(nb // 2, nb)
        rows = [pl.ds(r0, blk) for r0 in starts]
        cum = [_dot(summ, jnp.concatenate([lh_ref[r, :], ll_ref[r, :]], axis=0)) for r in rows]
        ops = [[None] * len(units) for _ in range(nb)]
        raw = [[None] * len(units) for _ in range(nb)]
        sc = [[None] * len(units) for _ in range(nb)]
        upd = [[None] * len(units) for _ in range(nb)]
        o = [[None] * len(units) for _ in range(nb)]

        def prepare(n, u):
            ops[n][u] = _scaled_operands(q_ref[rows[n], ksl[u]], k_ref[rows[n], ksl[u]],
                                         cum[n][:, ksl[u]], plan)

        def score(n, u):
            op = ops[n][u]
            raw[n][u] = [_dot_nt(block_diag(op["q0"]), op["k0"])] + [
                _dot_nt(block_diag(op["q_up"][l]), op["k_mix"][l]) for l in range(len(plan.levels))]

        def mask_and_update(n, u):
            sc[n][u] = [_masked_scores(
                [r[p * (r.shape[0] // pack):(p + 1) * (r.shape[0] // pack), :] for r in raw[n][u]],
                lvl, lvl_pc, plan) for p in members]
            upd[n][u] = _dot_tn(ops[n][u]["k_dec"], v_ref[rows[n], vsl[u]])

        def output(n, u):
            st = st_ref[u]
            lhs = jnp.concatenate(
                [jnp.concatenate([sc[n][u][p], ops[n][u]["q_dec"][:, kin[p]]], axis=1)
                 for p in members], axis=0)
            res = _dot(lhs, jnp.concatenate([v_ref[rows[n], vsl[u]], st.astype(BF16)], axis=0))
            o[n][u] = [res[p * blk:(p + 1) * blk, vin[p]] for p in members]
            for p in members:
                g_col = jnp.transpose(ops[n][u]["g_last"][:, kin[p]])
                st_ref[u, :, vin[p]] = g_col * st[:, vin[p]] + upd[n][u][kin[p], vin[p]]

        def finish(n, u):
            for p in members:
                vh = slice(vsl[u].start + p * dv, vsl[u].start + (p + 1) * dv)
                y = _norm_gate(o[n][u][p], zg_ref[rows[n], vh].astype(F32))
                y_ref[rows[n], vh] = y.astype(y_ref.dtype)

        for n in grp_a:
            for u in units:
                prepare(n, u)
        for n, m in zip(grp_a, grp_b):
            for u in units:
                score(n, u)
                prepare(m, u)
        for n in grp_a:
            for u in units:
                mask_and_update(n, u)
        for n, m in zip(grp_a, grp_b):
            for u in units:
                output(n, u)
                score(m, u)
        for n, m in zip(grp_a, grp_b):
            for u in units:
                finish(n, u)
                mask_and_update(m, u)
        for m in grp_b:
            for u in units:
                output(m, u)
        for m in grp_b:
            for u in units:
                finish(m, u)

    meta_rows = pl.ds(SEQ, N_META)

    @pl.when(pl.program_id(1) == 0)
    def _():
        meta_block()
        st_meta_ref[...] = st_ref[...]
        y_meta_ref[...] = y_ref[meta_rows, :]

    @pl.when(pl.program_id(1) > 0)
    def _():
        st_ref[...] = st_meta_ref[...]
        y_ref[meta_rows, :] = y_meta_ref[...]

    def step(i, carry):
        base = pl.multiple_of(i * (per * blk), per * blk)
        blocks([base + n * blk for n in range(per)])
        return carry

    lax.fori_loop(0, SEQ // (per * blk), step, 0)


def _attention(q, k, lg_hi, lg_lo, v, zg, *, heads, pack, per, blk, dk, dv, name):
    assert heads % pack == 0
    assert SEQ % (per * blk) == 0
    plan = _BlockPlan(blk)
    bsz, t, _ = q.shape
    n_groups = v.shape[2] // (heads * dv)
    summ, lvl = plan.consts()
    qk = pl.BlockSpec((None, t, heads * dk), lambda g, bi: (bi, 0, g))
    vz = pl.BlockSpec((None, t, heads * dv), lambda g, bi: (bi, 0, g))
    return pl.pallas_call(
        functools.partial(_attn_kernel, heads=heads, pack=pack, per=per, plan=plan),
        out_shape=jax.ShapeDtypeStruct(v.shape, BF16),
        grid=(n_groups, bsz),
        in_specs=[qk, qk, qk, qk, vz, vz,
                  pl.BlockSpec(summ.shape, lambda g, bi: (0, 0)),
                  pl.BlockSpec(lvl.shape, lambda g, bi: (0, 0))],
        out_specs=vz,
        scratch_shapes=[pltpu.VMEM((heads // pack, dk, pack * dv), F32),
                        pltpu.VMEM((heads // pack, dk, pack * dv), F32),
                        pltpu.VMEM((N_META, heads * dv), BF16)],
        compiler_params=pltpu.CompilerParams(
            dimension_semantics=("arbitrary", "arbitrary"), vmem_limit_bytes=VMEM_LIMIT_V7X),
        name=name,
    )(q, k, lg_hi, lg_lo, v, zg, summ, lvl)


PROJ_TM = 1376
PROJ_TN = 512
OPROJ_TM, OPROJ_TN, OPROJ_TK = 2064, 1024, 1024
LN_TR = 512
LN0_TR = 256
GATE_TN = 1024
HG_HEADS_PER_STEP = 4


def kernel(x, meta, lb_logits, l0_w_in, l0_b_f, l0_norm_g, l0_w_out, l0_ln_g, l0_ln_b,
           l1_w_in, l1_w_g1, l1_w_g2, l1_b_g, l1_norm_g, l1_w_out, l1_ln_g, l1_ln_b):
    bsz = x.shape[0]
    m = bsz * T_TOT
    a0 = _concat_cast(x, meta, tr=LN_TR).reshape(m, D_MODEL)
    s3 = (bsz, T_TOT, D_INNER)
    row = lambda vv: vv.reshape(1, -1)

    proj0 = functools.partial(_proj, a0, l0_w_in, tm=PROJ_TM, tn=PROJ_TN)
    q = proj0(0 * D_INNER, D_INNER, out_dtypes=(BF16,), epilogue=_ep_silu, name="l0_proj_q")
    lg_hi, lg_lo, k = proj0(1 * D_INNER, D_INNER, out_dtypes=(BF16, BF16, BF16),
                            epilogue=_ep_hgrn2_gate, vecs=(row(l0_b_f), lb_logits),
                            name="l0_proj_f")
    v = proj0(2 * D_INNER, D_INNER, out_dtypes=(BF16,), name="l0_proj_i")
    zg = proj0(3 * D_INNER, D_INNER, out_dtypes=(BF16,), epilogue=_ep_silu_gain,
               vecs=(row(l0_norm_g),), name="l0_proj_z")
    y = _attention(*(t.reshape(s3) for t in (q, k, lg_hi, lg_lo, v, zg)),
                   heads=HG_HEADS_PER_STEP, pack=2, per=8, blk=HGRN2_BLK, dk=HG_DK, dv=HG_DK, name="hgrn2_attention")
    br = _oproj(y.reshape(m, D_INNER), l0_w_out, tm=OPROJ_TM, tn=OPROJ_TN, tk=OPROJ_TK,
                name="l0_oproj")
    h1, a1 = _ln_first(x, meta, br.reshape(bsz, T_TOT, D_MODEL), l0_ln_g, l0_ln_b, tr=LN0_TR)
    a1 = a1.reshape(m, D_MODEL)

    proj1 = functools.partial(_proj, a1, l1_w_in, tm=PROJ_TM, tn=PROJ_TN)
    q = proj1(0, GLA_KW, out_dtypes=(BF16,),
              epilogue=functools.partial(_ep_scale, scale=GLA_DK ** -0.5), name="l1_proj_q")
    k = proj1(GLA_KW, GLA_KW, out_dtypes=(BF16,), name="l1_proj_k")
    v = proj1(2 * GLA_KW, D_INNER, out_dtypes=(BF16,), name="l1_proj_v")
    zg = proj1(2 * GLA_KW + D_INNER, D_INNER, out_dtypes=(BF16,), epilogue=_ep_silu_gain,
               vecs=(row(l1_norm_g),), name="l1_proj_z")
    w_g1p = jnp.pad(l1_w_g1, ((0, 0), (0, LANE - GLA_RANK)))
    w_g2p = jnp.pad(l1_w_g2, ((0, LANE - GLA_RANK), (0, 0)))
    lg_hi, lg_lo = _gla_gate(a1, w_g1p, w_g2p, row(l1_b_g), tm=PROJ_TM, tn=GATE_TN)
    s2 = (bsz, T_TOT, GLA_KW)
    y = _attention(q.reshape(s2), k.reshape(s2), lg_hi.reshape(s2), lg_lo.reshape(s2),
                   v.reshape(s3), zg.reshape(s3),
                   heads=1, pack=1, per=4, blk=GLA_BLK, dk=GLA_DK, dv=GLA_DV, name="gla_attention")
    br = _oproj(y.reshape(m, D_INNER), l1_w_out, tm=OPROJ_TM, tn=OPROJ_TN, tk=OPROJ_TK,
                name="l1_oproj")
    return _ln_seq_rows(h1, br.reshape(bsz, T_TOT, D_MODEL), l1_ln_g, l1_ln_b, tr=LN_TR,
                        name="l1_ln")
```

```python
import functools

import numpy as np
import jax
import jax.numpy as jnp
from jax import lax
from jax.experimental import pallas as pl
from jax.experimental.pallas import tpu as pltpu

D_MODEL = 4096
SEQ = 2048
N_META = 16
T_TOT = SEQ + N_META
D_INNER = 2 * D_MODEL
HG_DK = 128
GLA_HEADS = 8
GLA_KW = D_INNER // 2
GLA_DK = GLA_KW // GLA_HEADS
GLA_DV = D_INNER // GLA_HEADS
GLA_RANK = 16
GLA_GATE_NORM = 16.0
DEPTH = 2
ALPHA = (2.0 * DEPTH) ** 0.25
LN_EPS = 1e-5
RMS_EPS = 1e-6
LOG2_E = 1.4426950408889634

SUB = 16
HGRN2_BLK, GLA_BLK = 128, 256
LANE = 128
VMEM_LIMIT_V7X = 60 * 1024 * 1024

F32 = jnp.float32
BF16 = jnp.bfloat16


def _dot(a, b):
    return jnp.dot(a, b, preferred_element_type=F32)


def _dot_nt(a, b):
    return lax.dot_general(a, b, (((1,), (1,)), ((), ())), preferred_element_type=F32)


def _dot_tn(a, b):
    return lax.dot_general(a, b, (((0,), (0,)), ((), ())), preferred_element_type=F32)


def _sigmoid(x):
    return 0.5 + 0.5 * jnp.tanh(0.5 * x)


def _split_bf16(x):
    hi = x.astype(BF16)
    return hi, (x - hi.astype(F32)).astype(BF16)


def _ep_identity(acc):
    return (acc,)


def _ep_silu(acc):
    return (acc * _sigmoid(acc),)


def _ep_scale(acc, *, scale):
    return (acc * scale,)


def _ep_silu_gain(acc, gain):
    return (acc * _sigmoid(acc) * gain,)


def _ep_hgrn2_gate(acc, b_f, lb_logits):
    e = jnp.exp(lb_logits - jnp.max(lb_logits, axis=0, keepdims=True))
    lb = e[0:1, :] / jnp.sum(e, axis=0, keepdims=True)
    c = 0.5 * (1.0 - lb)
    ct = c * jnp.tanh(0.5 * (acc + b_f))
    hi, lo = _split_bf16(jnp.log2((lb + c) + ct))
    return hi, lo, c - ct


def _ep_gla_gate(acc, b_g):
    u = acc + b_g
    log_sig = jnp.minimum(u, 0.0) - jnp.log(1.0 + jnp.exp(-jnp.abs(u)))
    return _split_bf16(log_sig * (LOG2_E / GLA_GATE_NORM))


def _proj_kernel(a_ref, w_ref, *refs, epilogue, n_vec):
    vecs = [r[...] for r in refs[:n_vec]]
    acc = _dot(a_ref[...], w_ref[...].astype(BF16))
    for o_ref, val in zip(refs[n_vec:], epilogue(acc, *vecs)):
        o_ref[...] = val.astype(o_ref.dtype)


def _proj(a, w, col0, ncols, *, tm, tn, out_dtypes, epilogue=_ep_identity, vecs=(), name):
    m, k = a.shape
    assert m % tm == 0 and ncols % tn == 0 and col0 % tn == 0
    jb = col0 // tn
    outs = pl.pallas_call(
        functools.partial(_proj_kernel, epilogue=epilogue, n_vec=len(vecs)),
        out_shape=tuple(jax.ShapeDtypeStruct((m, ncols), dt) for dt in out_dtypes),
        grid=(m // tm, ncols // tn),
        in_specs=[
            pl.BlockSpec((tm, k), lambda i, j: (i, 0)),
            pl.BlockSpec((k, tn), lambda i, j: (0, j + jb)),
        ] + [pl.BlockSpec((vv.shape[0], tn), lambda i, j: (0, j)) for vv in vecs],
        out_specs=tuple(pl.BlockSpec((tm, tn), lambda i, j: (i, j)) for _ in out_dtypes),
        compiler_params=pltpu.CompilerParams(
            dimension_semantics=("arbitrary", "arbitrary"),
            vmem_limit_bytes=VMEM_LIMIT_V7X),
        name=name,
    )(a, w, *vecs)
    return outs if len(outs) > 1 else outs[0]


def _oproj_kernel(a_ref, w_ref, o_ref, acc_ref):
    kk = pl.program_id(2)

    @pl.when(kk == 0)
    def _():
        acc_ref[...] = _dot(a_ref[...], w_ref[...].astype(BF16))

    @pl.when(kk > 0)
    def _():
        acc_ref[...] += _dot(a_ref[...], w_ref[...].astype(BF16))

    @pl.when(kk == pl.num_programs(2) - 1)
    def _():
        o_ref[...] = acc_ref[...].astype(o_ref.dtype)


def _oproj(a, w, *, tm, tn, tk, name):
    m, k = a.shape
    n = w.shape[1]
    assert m % tm == 0 and n % tn == 0 and k % tk == 0
    return pl.pallas_call(
        _oproj_kernel,
        out_shape=jax.ShapeDtypeStruct((m, n), BF16),
        grid=(m // tm, n // tn, k // tk),
        in_specs=[
            pl.BlockSpec((tm, tk), lambda i, j, kk: (i, kk)),
            pl.BlockSpec((tk, tn), lambda i, j, kk: (kk, j)),
        ],
        out_specs=pl.BlockSpec((tm, tn), lambda i, j, kk: (i, j)),
        scratch_shapes=[pltpu.VMEM((tm, tn), F32)],
        compiler_params=pltpu.CompilerParams(
            dimension_semantics=("arbitrary", "arbitrary", "arbitrary"),
            vmem_limit_bytes=VMEM_LIMIT_V7X),
        name=name,
    )(a, w)


def _concat_cast_kernel(x_ref, meta_ref, o_ref):
    i = pl.program_id(1)

    @pl.when(i < pl.num_programs(1) - 1)
    def _():
        o_ref[...] = x_ref[...].astype(o_ref.dtype)

    @pl.when(i == pl.num_programs(1) - 1)
    def _():
        o_ref[0:N_META, :] = meta_ref[...].astype(o_ref.dtype)


def _concat_cast(x, meta, *, tr):
    bsz, _, d = x.shape
    n_x = SEQ // tr
    return pl.pallas_call(
        _concat_cast_kernel,
        out_shape=jax.ShapeDtypeStruct((bsz, T_TOT, d), BF16),
        grid=(bsz, n_x + 1),
        in_specs=[pl.BlockSpec((None, tr, d), lambda bi, i: (bi, jnp.minimum(i, n_x - 1), 0)),
                  pl.BlockSpec((N_META, d), lambda bi, i: (0, 0))],
        out_specs=pl.BlockSpec((None, tr, d), lambda bi, i: (bi, i, 0)),
        compiler_params=pltpu.CompilerParams(
            dimension_semantics=("arbitrary", "arbitrary"), vmem_limit_bytes=VMEM_LIMIT_V7X),
        name="concat_cast",
    )(x, meta)


def _ln_rows(h, br, g, b):
    r = ALPHA * h + br.astype(F32)
    mu = jnp.mean(r, axis=-1, keepdims=True)
    c = r - mu
    var = jnp.mean(c * c, axis=-1, keepdims=True)
    return c * lax.rsqrt(var + LN_EPS) * g + b


def _ln_first_kernel(x_ref, meta_ref, br_ref, g_ref, b_ref, o32_ref, o16_ref):
    i = pl.program_id(1)

    @pl.when(i < pl.num_programs(1) - 1)
    def _():
        y = _ln_rows(x_ref[...], br_ref[...], g_ref[...], b_ref[...])
        o32_ref[...] = y
        o16_ref[...] = y.astype(o16_ref.dtype)

    @pl.when(i == pl.num_programs(1) - 1)
    def _():
        y = _ln_rows(meta_ref[...], br_ref[0:N_META, :], g_ref[...], b_ref[...])
        o32_ref[0:N_META, :] = y
        o16_ref[0:N_META, :] = y.astype(o16_ref.dtype)


def _ln_first(x, meta, br, g, b, *, tr):
    bsz, _, d = x.shape
    n_x = SEQ // tr
    blk = pl.BlockSpec((None, tr, d), lambda bi, i: (bi, i, 0))
    vec = pl.BlockSpec((1, d), lambda bi, i: (0, 0))
    return pl.pallas_call(
        _ln_first_kernel,
        out_shape=(jax.ShapeDtypeStruct((bsz, T_TOT, d), F32),
                   jax.ShapeDtypeStruct((bsz, T_TOT, d), BF16)),
        grid=(bsz, n_x + 1),
        in_specs=[pl.BlockSpec((None, tr, d), lambda bi, i: (bi, jnp.minimum(i, n_x - 1), 0)),
                  pl.BlockSpec((N_META, d), lambda bi, i: (0, 0)), blk, vec, vec],
        out_specs=(blk, blk),
        compiler_params=pltpu.CompilerParams(
            dimension_semantics=("arbitrary", "arbitrary"), vmem_limit_bytes=VMEM_LIMIT_V7X),
        name="l0_ln",
    )(x, meta, br, g.reshape(1, d), b.reshape(1, d))


def _ln_kernel(h_ref, br_ref, g_ref, b_ref, o_ref):
    o_ref[...] = _ln_rows(h_ref[...], br_ref[...], g_ref[...], b_ref[...])


def _ln_seq_rows(h, br, g, b, *, tr, name):
    bsz, _, d = h.shape
    assert SEQ % tr == 0
    row = pl.BlockSpec((None, tr, d), lambda bi, i: (bi, i, 0))
    vec = pl.BlockSpec((1, d), lambda bi, i: (0, 0))
    return pl.pallas_call(
        _ln_kernel,
        out_shape=jax.ShapeDtypeStruct((bsz, SEQ, d), F32),
        grid=(bsz, SEQ // tr),
        in_specs=[row, row, vec, vec],
        out_specs=row,
        compiler_params=pltpu.CompilerParams(
            dimension_semantics=("arbitrary", "arbitrary"), vmem_limit_bytes=VMEM_LIMIT_V7X),
        name=name,
    )(h, br, g.reshape(1, d), b.reshape(1, d))


def _is_upper(p, g):
    return (p * SUB) % g >= g // 2


def _chunk_end_of_boundary(p, g):
    return ((p * SUB) // g * g + g // 2) // SUB - 1


def _row_scalar_table(n_sub, levels):
    pairs, index = [], {}

    def add(key, plus, minus):
        index[key] = len(pairs)
        pairs.append((plus, minus))

    last = n_sub - 1
    for p in range(1, n_sub):
        add(("q_dec", p), p - 1, None)
    for p in range(last):
        add(("k_dec", p), last, p)
    add(("g_last",), last, None)
    for g in levels:
        for p in range(n_sub):
            m = _chunk_end_of_boundary(p, g)
            if _is_upper(p, g) and m != p - 1:
                add(("up", g, p), p - 1, m)
            if not _is_upper(p, g) and m != p:
                add(("lo", g, p), m, p)
    return index, pairs


class _BlockPlan:
    def __init__(self, blk):
        assert blk % (2 * SUB) == 0 and blk & (blk - 1) == 0
        self.blk = blk
        self.n_sub = blk // SUB
        self.levels = tuple(2 * SUB << i for i in range((blk // (2 * SUB)).bit_length()))
        self.scalar_row, self.pairs = _row_scalar_table(self.n_sub, self.levels)
        self.n_scalar_rows = -(-len(self.pairs) // SUB) * SUB

    def consts(self):
        c = self.blk
        i = np.arange(c)[:, None]
        s = np.arange(c)[None, :]
        summ = np.zeros((c + self.n_scalar_rows, c), np.float32)
        summ[:c] = s <= i
        for r, (plus, minus) in enumerate(self.pairs):
            if plus is not None:
                summ[c + r, :SUB * (plus + 1)] += 1.0
            if minus is not None:
                summ[c + r, :SUB * (minus + 1)] -= 1.0
        lvl = np.full((c, c), -1, np.int32)
        lvl[((i // SUB) == (s // SUB)) & (s <= i)] = 0
        for l, g in enumerate(self.levels):
            half = g // 2
            lvl[((i // g) == (s // g)) & ((i % g) >= half) & ((s % g) < half)] = l + 1
        return jnp.asarray(np.concatenate([summ, summ], axis=1), BF16), jnp.asarray(lvl)


def _pieces(x):
    return [x[p * SUB:(p + 1) * SUB, :] for p in range(x.shape[0] // SUB)]


def _rows(b, r, n):
    return jnp.broadcast_to(b[r:r + 1, :], (n, b.shape[1]))


def _scaled_operands(q, k, cum, plan):
    n = q.shape[1]
    n_sub, blk = plan.n_sub, plan.blk
    b = cum[:blk]
    b_pc = _pieces(b)
    ends = [_rows(b, (p + 1) * SUB - 1, SUB) for p in range(n_sub)]
    e0 = jnp.concatenate([b_pc[0]] + [b_pc[p] - ends[p - 1] for p in range(1, n_sub)], axis=0)
    to_end = jnp.concatenate([ends[p] - b_pc[p] for p in range(n_sub)], axis=0)
    scal = jnp.exp2(cum[blk:])

    def factor(key):
        r = plan.scalar_row[key]
        return jnp.broadcast_to(scal[r:r + 1, :], (SUB, n)).astype(BF16)

    def scaled(piece, key):
        return piece * factor(key) if key in plan.scalar_row else piece

    q0 = q * jnp.exp2(e0).astype(BF16)
    k0 = k * jnp.exp2(-e0).astype(BF16)
    ky = k * jnp.exp2(to_end).astype(BF16)
    q0_pc, k_pc, ky_pc = _pieces(q0), _pieces(k), _pieces(ky)
    ops = {"q0": q0, "k0": k0, "q_up": [], "k_mix": []}
    for g in plan.levels:
        ops["q_up"].append(jnp.concatenate(
            [scaled(q0_pc[p], ("up", g, p)) for p in range(n_sub) if _is_upper(p, g)], axis=0))
        ops["k_mix"].append(jnp.concatenate(
            [k_pc[p] if _is_upper(p, g) else scaled(ky_pc[p], ("lo", g, p))
             for p in range(n_sub)], axis=0))
    ops["q_dec"] = jnp.concatenate([scaled(q0_pc[p], ("q_dec", p)) for p in range(n_sub)], axis=0)
    ops["k_dec"] = jnp.concatenate([scaled(ky_pc[p], ("k_dec", p)) for p in range(n_sub)], axis=0)
    r = plan.scalar_row[("g_last",)]
    ops["g_last"] = scal[r:r + 1, :]
    return ops


def _masked_scores(raw, lvl, lvl_pc, plan):
    sc_pc = _pieces(jnp.where(lvl == 0, raw[0], 0.0))
    for l, g in enumerate(plan.levels):
        ups = [p for p in range(plan.n_sub) if _is_upper(p, g)]
        for j, p in enumerate(ups):
            sc_pc[p] = jnp.where(lvl_pc[p] == l + 1, raw[l + 1][j * SUB:(j + 1) * SUB, :], sc_pc[p])
    return jnp.concatenate(sc_pc, axis=0).astype(BF16)


def _norm_gate(o, zg):
    ms = jnp.mean(o * o, axis=-1, keepdims=True)
    return o * lax.rsqrt(ms + RMS_EPS) * zg


def _attn_kernel(q_ref, k_ref, lh_ref, ll_ref, v_ref, zg_ref, sum_ref, lvl_ref, y_ref, st_ref,
                 st_meta_ref, y_meta_ref, *, heads, pack, per, plan):
    blk = plan.blk
    lvl = lvl_ref[...]
    lvl_pc = _pieces(lvl)
    summ = sum_ref[...]
    dk = q_ref.shape[1] // heads
    dv = v_ref.shape[1] // heads
    units = range(heads // pack)
    members = range(pack)
    ksl = [slice(u * pack * dk, (u + 1) * pack * dk) for u in units]
    vsl = [slice(u * pack * dv, (u + 1) * pack * dv) for u in units]
    kin = [slice(p * dk, (p + 1) * dk) for p in members]
    vin = [slice(p * dv, (p + 1) * dv) for p in members]

    def block_diag(x):
        if pack == 1:
            return x
        zero = jnp.zeros((x.shape[0], dk), x.dtype)
        return jnp.concatenate(
            [jnp.concatenate([x[:, kin[p]] if c == p else zero for c in members], axis=1)
             for p in members], axis=0)

    def pad_rows(t):
        return jnp.concatenate([t, jnp.zeros((blk - N_META, t.shape[1]), t.dtype)], axis=0)

    def meta_block():
        rows = pl.ds(SEQ, N_META)
        b = _dot(summ[0:N_META, :], jnp.concatenate(
            [pad_rows(lh_ref[rows, :]), pad_rows(ll_ref[rows, :])], axis=0))
        q, k = q_ref[rows, :], k_ref[rows, :]
        q0 = q * jnp.exp2(b).astype(BF16)
        k0 = pad_rows(k * jnp.exp2(-b).astype(BF16))
        k_dec = pad_rows(k * jnp.exp2(b[N_META - 1:N_META, :] - b).astype(BF16))
        v = pad_rows(v_ref[rows, :])
        for u in units:
            for p in members:
                kh = slice(ksl[u].start + p * dk, ksl[u].start + (p + 1) * dk)
                vh = slice(vsl[u].start + p * dv, vsl[u].start + (p + 1) * dv)
                sc = jnp.where(lvl[0:N_META, :] == 0, _dot_nt(q0[:, kh], k0[:, kh]), 0.0)
                o = _dot(sc.astype(BF16), v[:, vh])
                st_ref[u, :, vin[p]] = _dot_tn(k_dec[:, kh], v[:, vh])
                y_ref[rows, vh] = _norm_gate(o, zg_ref[rows, vh].astype(F32)).astype(y_ref.dtype)

    def blocks(starts):
        nb = len(starts)
        grp_a, grp_b = range(nb // 2), range(nb // 2, nb)
        rows = [pl.ds(r0, blk) for r0 in starts]
        cum = [_dot(summ, jnp.concatenate([lh_ref[r, :], ll_ref[r, :]], axis=0)) for r in rows]
        ops = [[None] * len(units) for _ in range(nb)]
        raw = [[None] * len(units) for _ in range(nb)]
        sc = [[None] * len(units) for _ in range(nb)]
        upd = [[None] * len(units) for _ in range(nb)]
        o = [[None] * len(units) for _ in range(nb)]

        def prepare(n, u):
            ops[n][u] = _scaled_operands(q_ref[rows[n], ksl[u]], k_ref[rows[n], ksl[u]],
                                         cum[n][:, ksl[u]], plan)

        def score(n, u):
            op = ops[n][u]
            raw[n][u] = [_dot_nt(block_diag(op["q0"]), op["k0"])] + [
                _dot_nt(block_diag(op["q_up"][l]), op["k_mix"][l]) for l in range(len(plan.levels))]

        def mask_and_update(n, u):
            sc[n][u] = [_masked_scores(
                [r[p * (r.shape[0] // pack):(p + 1) * (r.shape[0] // pack), :] for r in raw[n][u]],
                lvl, lvl_pc, plan) for p in members]
            upd[n][u] = _dot_tn(ops[n][u]["k_dec"], v_ref[rows[n], vsl[u]])

        def output(n, u):
            st = st_ref[u]
            lhs = jnp.concatenate(
                [jnp.concatenate([sc[n][u][p], ops[n][u]["q_dec"][:, kin[p]]], axis=1)
                 for p in members], axis=0)
            res = _dot(lhs, jnp.concatenate([v_ref[rows[n], vsl[u]], st.astype(BF16)], axis=0))
            o[n][u] = [res[p * blk:(p + 1) * blk, vin[p]] for p in members]
            for p in members:
                g_col = jnp.transpose(ops[n][u]["g_last"][:, kin[p]])
                st_ref[u, :, vin[p]] = g_col * st[:, vin[p]] + upd[n][u][kin[p], vin[p]]

        def finish(n, u):
            for p in members:
                vh = slice(vsl[u].start + p * dv, vsl[u].start + (p + 1) * dv)
                y = _norm_gate(o[n][u][p], zg_ref[rows[n], vh].astype(F32))
                y_ref[rows[n], vh] = y.astype(y_ref.dtype)

        for n in grp_a:
            for u in units:
                prepare(n, u)
        for n, m in zip(grp_a, grp_b):
            for u in units:
                score(n, u)
                prepare(m, u)
        for n in grp_a:
            for u in units:
                mask_and_update(n, u)
        for n, m in zip(grp_a, grp_b):
            for u in units:
                output(n, u)
                score(m, u)
        for n, m in zip(grp_a, grp_b):
            for u in units:
                finish(n, u)
                mask_and_update(m, u)
        for m in grp_b:
            for u in units:
                output(m, u)
        for m in grp_b:
            for u in units:
                finish(m, u)

    meta_rows = pl.ds(SEQ, N_META)

    @pl.when(pl.program_id(1) == 0)
    def _():
        meta_block()
        st_meta_ref[...] = st_ref[...]
        y_meta_ref[...] = y_ref[meta_rows, :]

    @pl.when(pl.program_id(1) > 0)
    def _():
        st_ref[...] = st_meta_ref[...]
        y_ref[meta_rows, :] = y_meta_ref[...]

    def step(i, carry):
        base = pl.multiple_of(i * (per * blk), per * blk)
        blocks([base + n * blk for n in range(per)])
        return carry

    lax.fori_loop(0, SEQ // (per * blk), step, 0)


def _attention(q, k, lg_hi, lg_lo, v, zg, *, heads, pack, per, blk, dk, dv, name):
    assert heads % pack == 0
    assert SEQ % (per * blk) == 0
    plan = _BlockPlan(blk)
    bsz, t, _ = q.shape
    n_groups = v.shape[2] // (heads * dv)
    summ, lvl = plan.consts()
    qk = pl.BlockSpec((None, t, heads * dk), lambda g, bi: (bi, 0, g))
    vz = pl.BlockSpec((None, t, heads * dv), lambda g, bi: (bi, 0, g))
    return pl.pallas_call(
        functools.partial(_attn_kernel, heads=heads, pack=pack, per=per, plan=plan),
        out_shape=jax.ShapeDtypeStruct(v.shape, BF16),
        grid=(n_groups, bsz),
        in_specs=[qk, qk, qk, qk, vz, vz,
                  pl.BlockSpec(summ.shape, lambda g, bi: (0, 0)),
                  pl.BlockSpec(lvl.shape, lambda g, bi: (0, 0))],
        out_specs=vz,
        scratch_shapes=[pltpu.VMEM((heads // pack, dk, pack * dv), F32),
                        pltpu.VMEM((heads // pack, dk, pack * dv), F32),
                        pltpu.VMEM((N_META, heads * dv), BF16)],
        compiler_params=pltpu.CompilerParams(
            dimension_semantics=("arbitrary", "arbitrary"), vmem_limit_bytes=VMEM_LIMIT_V7X),
        name=name,
    )(q, k, lg_hi, lg_lo, v, zg, summ, lvl)


PROJ_TM = 1376
PROJ_TN = 512
OPROJ_TM, OPROJ_TN, OPROJ_TK = 2064, 1024, 1024
LN_TR = 512
LN0_TR = 256
GATE_TN = 1024
HG_HEADS_PER_STEP = 4


def kernel(x, meta, lb_logits, l0_w_in, l0_b_f, l0_norm_g, l0_w_out, l0_ln_g, l0_ln_b,
           l1_w_in, l1_w_g1, l1_w_g2, l1_b_g, l1_norm_g, l1_w_out, l1_ln_g, l1_ln_b):
    bsz = x.shape[0]
    m = bsz * T_TOT
    a0 = _concat_cast(x, meta, tr=LN_TR).reshape(m, D_MODEL)
    s3 = (bsz, T_TOT, D_INNER)
    row = lambda vv: vv.reshape(1, -1)

    proj0 = functools.partial(_proj, a0, l0_w_in, tm=PROJ_TM, tn=PROJ_TN)
    q = proj0(0 * D_INNER, D_INNER, out_dtypes=(BF16,), epilogue=_ep_silu, name="l0_proj_q")
    lg_hi, lg_lo, k = proj0(1 * D_INNER, D_INNER, out_dtypes=(BF16, BF16, BF16),
                            epilogue=_ep_hgrn2_gate, vecs=(row(l0_b_f), lb_logits),
                            name="l0_proj_f")
    v = proj0(2 * D_INNER, D_INNER, out_dtypes=(BF16,), name="l0_proj_i")
    zg = proj0(3 * D_INNER, D_INNER, out_dtypes=(BF16,), epilogue=_ep_silu_gain,
               vecs=(row(l0_norm_g),), name="l0_proj_z")
    y = _attention(*(t.reshape(s3) for t in (q, k, lg_hi, lg_lo, v, zg)),
                   heads=HG_HEADS_PER_STEP, pack=2, per=8, blk=HGRN2_BLK, dk=HG_DK, dv=HG_DK, name="hgrn2_attention")
    br = _oproj(y.reshape(m, D_INNER), l0_w_out, tm=OPROJ_TM, tn=OPROJ_TN, tk=OPROJ_TK,
                name="l0_oproj")
    h1, a1 = _ln_first(x, meta, br.reshape(bsz, T_TOT, D_MODEL), l0_ln_g, l0_ln_b, tr=LN0_TR)
    a1 = a1.reshape(m, D_MODEL)

    proj1 = functools.partial(_proj, a1, l1_w_in, tm=PROJ_TM, tn=PROJ_TN)
    q = proj1(0, GLA_KW, out_dtypes=(BF16,),
              epilogue=functools.partial(_ep_scale, scale=GLA_DK ** -0.5), name="l1_proj_q")
    k = proj1(GLA_KW, GLA_KW, out_dtypes=(BF16,), name="l1_proj_k")
    v = proj1(2 * GLA_KW, D_INNER, out_dtypes=(BF16,), name="l1_proj_v")
    zg = proj1(2 * GLA_KW + D_INNER, D_INNER, out_dtypes=(BF16,), epilogue=_ep_silu_gain,
               vecs=(row(l1_norm_g),), name="l1_proj_z")
    w_g1p = jnp.pad(l1_w_g1, ((0, 0), (0, LANE - GLA_RANK)))
    w_g2p = jnp.pad(l1_w_g2, ((0, LANE - GLA_RANK), (0, 0)))
    g1 = _proj(a1, w_g1p, 0, LANE, tm=PROJ_TM, tn=LANE, out_dtypes=(BF16,), name="l1_proj_g1")
    lg_hi, lg_lo = _proj(g1, w_g2p, 0, GLA_KW, tm=PROJ_TM, tn=GATE_TN, out_dtypes=(BF16, BF16),
                         epilogue=_ep_gla_gate, vecs=(row(l1_b_g),), name="l1_gate")
    s2 = (bsz, T_TOT, GLA_KW)
    y = _attention(q.reshape(s2), k.reshape(s2), lg_hi.reshape(s2), lg_lo.reshape(s2),
                   v.reshape(s3), zg.reshape(s3),
                   heads=1, pack=1, per=4, blk=GLA_BLK, dk=GLA_DK, dv=GLA_DV, name="gla_attention")
    br = _oproj(y.reshape(m, D_INNER), l1_w_out, tm=OPROJ_TM, tn=OPROJ_TN, tk=OPROJ_TK,
                name="l1_oproj")
    return _ln_seq_rows(h1, br.reshape(bsz, T_TOT, D_MODEL), l1_ln_g, l1_ln_b, tr=LN_TR,
                        name="l1_ln")
```
